```python
import math, functools
import jax, jax.numpy as jnp
from jax import lax
import numpy as np

D_MODEL = 4096
BATCH = 4
SEQ = 2048
DEPTH = 2
DEC_BATCH = 8
DEC_SEQ = 8
PAST_LEN = 16384
PAGE_SIZE = 128

N_META = 16
H_A = 8
DV_A = D_MODEL // 32
DH_A = DV_A // 2
D_A = H_A * DV_A
H_B = 4
DV_B = D_MODEL // 16
DK_B = DV_B // 2
D_B = H_B * DV_B
GLA_RANK = 16
GLA_TAU = 16.0
GLA_CHUNK = 64
D_LRU = D_MODEL // 2
LRU_BLOCKS = 16
LRU_BW = D_LRU // LRU_BLOCKS
LRU_C = 8.0
CONV_W = 4
D_MIX = D_A + D_B + D_LRU
D_FF = 4 * D_MODEL
Q_BLOCK = 128
EPS = 1e-6
IN_SIZES = (H_A * 2 * DH_A, H_A * 2 * DH_A, D_A,
            H_B * DK_B, H_B * DK_B, D_B, D_B, GLA_RANK,
            D_LRU, D_LRU)
IN_SPLITS = tuple(int(c) for c in np.cumsum(IN_SIZES)[:-1])
N_IN = sum(IN_SIZES)

kernel_name = 'hymba_diff_gla_rglru_decode_step'


def rms_norm(x, g):
    xf = x.astype(jnp.float32)
    y = xf * lax.rsqrt(jnp.mean(xf * xf, axis=-1, keepdims=True) + EPS)
    return (y * g.astype(jnp.float32)).astype(x.dtype)


def alibi_slopes(n):
    return jnp.exp2(-8.0 * jnp.arange(1, n + 1, dtype=jnp.float32) / n)


def diff_attn_block(q, q_pos, k, v, k_pos, lam):
    s = jnp.einsum('bqhcd,bkhcd->bhcqk', q, k, preferred_element_type=jnp.float32) * (DH_A ** -0.5)
    dist = (q_pos[:, None] - k_pos[None, :]).astype(jnp.float32)
    bias = -alibi_slopes(H_A)[:, None, None] * dist
    s = jnp.where(dist >= 0, s + bias[None, :, None], -jnp.inf)
    p = jax.nn.softmax(s, axis=-1)
    w = p[:, :, 0] - lam * p[:, :, 1]
    return jnp.einsum('bhqk,bkhd->bqhd', w.astype(v.dtype), v)


def diff_attn_prompt(q, k, v, lam):
    B, T = q.shape[0], q.shape[1]
    n_blk = -(-T // Q_BLOCK)
    t_pad = n_blk * Q_BLOCK
    q_pad = jnp.pad(q, ((0, 0), (0, t_pad - T), (0, 0), (0, 0), (0, 0)))
    qb = q_pad.reshape(B, n_blk, Q_BLOCK, H_A, 2, DH_A).transpose(1, 0, 2, 3, 4, 5)
    pb = jnp.arange(t_pad, dtype=jnp.int32).reshape(n_blk, Q_BLOCK)
    k_pos = jnp.arange(T, dtype=jnp.int32)
    o = lax.map(lambda a: diff_attn_block(a[0], a[1], k, v, k_pos, lam), (qb, pb))
    return o.transpose(1, 0, 2, 3, 4).reshape(B, t_pad, H_A, DV_A)[:, :T]


def diff_attn_sample(q, k, v, lam, cache_k, cache_v, page_table, layer):
    db, s_dec = q.shape[0], q.shape[1]
    past_len = page_table.shape[1] * cache_k.shape[2]
    past_k = cache_k[layer, page_table].reshape(db, past_len, H_A, 2, DH_A)
    past_v = cache_v[layer, page_table].reshape(db, past_len, H_A, DV_A)
    keys = jnp.concatenate([past_k.astype(k.dtype), k], axis=1)
    vals = jnp.concatenate([past_v.astype(v.dtype), v], axis=1)
    k_pos = jnp.arange(past_len + s_dec, dtype=jnp.int32)
    q_pos = past_len + jnp.arange(s_dec, dtype=jnp.int32)
    return diff_attn_block(q, q_pos, keys, vals, k_pos, lam)


def gla_chunked(q, k, v, gk, s0, L):
    B, T = q.shape[0], q.shape[1]
    n = T // L

    def to_chunks(t):
        return t.reshape(B, n, L, H_B, t.shape[-1]).transpose(1, 0, 3, 2, 4).astype(jnp.float32)

    tril = jnp.tril(jnp.ones((L, L), dtype=bool))

    def step(S, inp):
        qc, kc, vc, gc = inp
        b = jnp.cumsum(gc, axis=2)
        b_last = b[:, :, -1:, :]
        qt = qc * jnp.exp(b)
        att = jnp.einsum('bhld,bhmd->bhlm', qt, kc * jnp.exp(-b))
        att = jnp.where(tril, att, 0.0)
        o = jnp.einsum('bhlm,bhme->bhle', att, vc) + jnp.einsum('bhld,bhde->bhle', qt, S)
        S = jnp.exp(b_last)[:, :, 0, :, None] * S + jnp.einsum('bhld,bhle->bhde', kc * jnp.exp(b_last - b), vc)
        return S, o

    S, o = lax.scan(step, s0.astype(jnp.float32), (to_chunks(q), to_chunks(k), to_chunks(v), to_chunks(gk)))
    return o.transpose(1, 0, 3, 2, 4).reshape(B, T, H_B, DV_B), S


def gla_mixer(q, k, v, g, r, p, s0, segments):
    B, T = q.shape[0], q.shape[1]
    q = q.reshape(B, T, H_B, DK_B) * (DK_B ** -0.5)
    k = k.reshape(B, T, H_B, DK_B)
    v = v.reshape(B, T, H_B, DV_B)
    gk = jax.nn.log_sigmoid((r @ p['gla_w_gate'] + p['gla_b_gate']).astype(jnp.float32)) / GLA_TAU
    gk = gk.reshape(B, T, H_B, DK_B)
    outs = []
    S = s0
    for a, b, L in segments:
        o, S = gla_chunked(q[:, a:b], k[:, a:b], v[:, a:b], gk[:, a:b], S, L)
        outs.append(o)
    o = rms_norm(jnp.concatenate(outs, axis=1), p['gla_norm'])
    return o.reshape(B, T, D_B).astype(g.dtype) * jax.nn.silu(g), S


def rglru_mixer(xb, gb, p, h0, buf):
    B, T = xb.shape[0], xb.shape[1]
    ext = jnp.concatenate([buf.astype(xb.dtype), xb], axis=1)
    u = lax.conv_general_dilated(ext, p['lru_conv_w'][:, None, :].astype(xb.dtype), window_strides=(1,),
                                 padding='VALID', dimension_numbers=('NWC', 'WIO', 'NWC'),
                                 feature_group_count=D_LRU) + p['lru_conv_b']
    new_buf = ext[:, -(CONV_W - 1):]
    ub = u.reshape(B, T, LRU_BLOCKS, LRU_BW)
    r = jax.nn.sigmoid(jnp.einsum('btnc,ncd->btnd', ub, p['lru_w_a']).reshape(B, T, D_LRU) + p['lru_b_a'])
    i = jax.nn.sigmoid(jnp.einsum('btnc,ncd->btnd', ub, p['lru_w_x']).reshape(B, T, D_LRU) + p['lru_b_x'])
    log_a = -LRU_C * r.astype(jnp.float32) * jax.nn.softplus(-p['lru_lambda'].astype(jnp.float32))
    a = jnp.exp(log_a)
    xin = jnp.sqrt(-jnp.expm1(2.0 * log_a)) * (i * u).astype(jnp.float32)

    def step(h, inp):
        a_t, x_t = inp
        h = a_t * h + x_t
        return h, h

    h_last, hs = lax.scan(step, h0.astype(jnp.float32), (a.swapaxes(0, 1), xin.swapaxes(0, 1)))
    y = hs.swapaxes(0, 1).astype(xb.dtype) * jax.nn.gelu(gb)
    return y, h_last, new_buf


def trunk_layer(h, p, lam, lam_init, attend, gla_s0, gla_segments, lru_h0, lru_buf):
    B, T = h.shape[0], h.shape[1]
    hn = rms_norm(h, p['norm_mix_pre'])
    z = hn @ p['w_in']
    aq, ak, av, bq, bk, bv, bg, br, cx, cg = jnp.split(z, IN_SPLITS, axis=-1)
    aq = aq.reshape(B, T, H_A, 2, DH_A)
    ak = ak.reshape(B, T, H_A, 2, DH_A)
    av = av.reshape(B, T, H_A, DV_A)
    oa = attend(aq, ak, av, lam)
    oa = (rms_norm(oa, p['attn_subln']) * (1.0 - lam_init)).reshape(B, T, D_A)
    ob, s_gla = gla_mixer(bq, bk, bv, bg, br, p, gla_s0, gla_segments)
    oc, h_lru, conv_buf = rglru_mixer(cx, cg, p, lru_h0, lru_buf)
    mix = jnp.concatenate([oa, ob, oc], axis=-1) @ p['w_out']
    h = h + rms_norm(mix, p['norm_mix_post'])
    m = rms_norm(h, p['norm_mlp_pre'])
    m = jnp.square(jax.nn.relu(m @ p['mlp_w_up'])) @ p['mlp_w_down']
    h = h + rms_norm(m, p['norm_mlp_post'])
    new_state = (ak.reshape(B, T, H_A, 2 * DH_A), av, s_gla.astype(h.dtype), h_lru.astype(h.dtype), conv_buf)
    return h, new_state


def setup_inputs(seed: int = 0) -> dict:
    key = jax.random.key(seed)
    ks = jax.random.split(key, 32)
    f32 = jnp.float32
    n_pages = PAST_LEN // PAGE_SIZE
    n_used = DEC_BATCH * n_pages
    n_pool = n_used + max(1, n_used // 4)

    def nrm(k, shape, scale):
        return jax.random.normal(k, shape, f32) * scale

    def gain(k, shape):
        return 1.0 + 0.01 * jax.random.normal(k, shape, f32)

    a0 = jax.random.uniform(ks[29], (DEPTH, D_LRU), f32, 0.9, 0.999)
    page_table = jax.random.permutation(ks[7], n_pool)[:n_used].reshape(DEC_BATCH, n_pages).astype(jnp.int32)
    return {
        'x_prompt': nrm(ks[0], (BATCH, SEQ, D_MODEL), 1.0),
        'x_sample': nrm(ks[1], (DEC_BATCH, DEC_SEQ, D_MODEL), 1.0),
        'cache_k': nrm(ks[2], (DEPTH, n_pool, PAGE_SIZE, H_A, 2 * DH_A), 1.0),
        'cache_v': nrm(ks[3], (DEPTH, n_pool, PAGE_SIZE, H_A, DV_A), 1.0),
        'state_gla': nrm(ks[4], (DEPTH, DEC_BATCH, H_B, DK_B, DV_B), 1.0),
        'state_lru_h': nrm(ks[5], (DEPTH, DEC_BATCH, D_LRU), 0.5),
        'state_lru_conv': nrm(ks[6], (DEPTH, DEC_BATCH, CONV_W - 1, D_LRU), 1.0),
        'page_table': page_table,
        'meta_tokens': nrm(ks[8], (N_META, D_MODEL), 1.0),
        'norm_mix_pre': gain(ks[9], (DEPTH, D_MODEL)),
        'norm_mix_post': gain(ks[10], (DEPTH, D_MODEL)),
        'norm_mlp_pre': gain(ks[11], (DEPTH, D_MODEL)),
        'norm_mlp_post': gain(ks[12], (DEPTH, D_MODEL)),
        'w_in': nrm(ks[13], (DEPTH, D_MODEL, N_IN), D_MODEL ** -0.5),
        'w_out': nrm(ks[14], (DEPTH, D_MIX, D_MODEL), D_MIX ** -0.5),
        'lam_q1': nrm(ks[15], (DEPTH, DH_A), 0.1),
        'lam_k1': nrm(ks[16], (DEPTH, DH_A), 0.1),
        'lam_q2': nrm(ks[17], (DEPTH, DH_A), 0.1),
        'lam_k2': nrm(ks[18], (DEPTH, DH_A), 0.1),
        'attn_subln': gain(ks[19], (DEPTH, DV_A)),
        'gla_w_gate': nrm(ks[20], (DEPTH, GLA_RANK, H_B * DK_B), GLA_RANK ** -0.5),
        'gla_b_gate': nrm(ks[21], (DEPTH, H_B * DK_B), 0.1),
        'gla_norm': gain(ks[22], (DEPTH, DV_B)),
        'lru_conv_w': nrm(ks[23], (DEPTH, CONV_W, D_LRU), CONV_W ** -0.5),
        'lru_conv_b': nrm(ks[24], (DEPTH, D_LRU), 0.01),
        'lru_w_a': nrm(ks[25], (DEPTH, LRU_BLOCKS, LRU_BW, LRU_BW), LRU_BW ** -0.5),
        'lru_b_a': nrm(ks[26], (DEPTH, D_LRU), 0.01),
        'lru_w_x': nrm(ks[27], (DEPTH, LRU_BLOCKS, LRU_BW, LRU_BW), LRU_BW ** -0.5),
        'lru_b_x': nrm(ks[28], (DEPTH, D_LRU), 0.01),
        'lru_lambda': jnp.log(a0) - jnp.log1p(-a0),
        'mlp_w_up': nrm(ks[30], (DEPTH, D_MODEL, D_FF), D_MODEL ** -0.5),
        'mlp_w_down': nrm(ks[31], (DEPTH, D_FF, D_MODEL), D_FF ** -0.5),
    }


def reference(x_prompt, x_sample, cache_k, cache_v, state_gla, state_lru_h, state_lru_conv, page_table,
              meta_tokens, norm_mix_pre, norm_mix_post, norm_mlp_pre, norm_mlp_post, w_in, w_out,
              lam_q1, lam_k1, lam_q2, lam_k2, attn_subln, gla_w_gate, gla_b_gate, gla_norm,
              lru_conv_w, lru_conv_b, lru_w_a, lru_b_a, lru_w_x, lru_b_x, lru_lambda, mlp_w_up, mlp_w_down):
    f32 = jnp.float32
    B, s_len = x_prompt.shape[0], x_prompt.shape[1]
    s_dec = x_sample.shape[1]
    T = N_META + s_len
    h_p = jnp.concatenate([jnp.broadcast_to(meta_tokens[None].astype(x_prompt.dtype), (B, N_META, D_MODEL)),
                           x_prompt], axis=1)
    h_s = x_sample
    prompt_segments = ((0, N_META, N_META), (N_META, T, math.gcd(s_len, GLA_CHUNK)))
    sample_segments = ((0, s_dec, math.gcd(s_dec, GLA_CHUNK)),)
    gla_zero = jnp.zeros((B, H_B, DK_B, DV_B), f32)
    lru_zero = jnp.zeros((B, D_LRU), f32)
    buf_zero = jnp.zeros((B, CONV_W - 1, D_LRU), x_prompt.dtype)
    kp, vp, ksm, vsm, gp, gs, hp, hs, cp, cs = ([] for _ in range(10))
    for l in range(DEPTH):
        p = {'norm_mix_pre': norm_mix_pre[l], 'norm_mix_post': norm_mix_post[l],
             'norm_mlp_pre': norm_mlp_pre[l], 'norm_mlp_post': norm_mlp_post[l],
             'w_in': w_in[l], 'w_out': w_out[l], 'attn_subln': attn_subln[l],
             'gla_w_gate': gla_w_gate[l], 'gla_b_gate': gla_b_gate[l], 'gla_norm': gla_norm[l],
             'lru_conv_w': lru_conv_w[l], 'lru_conv_b': lru_conv_b[l], 'lru_w_a': lru_w_a[l],
             'lru_b_a': lru_b_a[l], 'lru_w_x': lru_w_x[l], 'lru_b_x': lru_b_x[l],
             'lru_lambda': lru_lambda[l], 'mlp_w_up': mlp_w_up[l], 'mlp_w_down': mlp_w_down[l]}
        lam_init = 0.8 - 0.6 * math.exp(-0.3 * l)
        lam = (jnp.exp(jnp.sum(lam_q1[l].astype(f32) * lam_k1[l].astype(f32)))
               - jnp.exp(jnp.sum(lam_q2[l].astype(f32) * lam_k2[l].astype(f32))) + lam_init)
        h_p, st_p = trunk_layer(h_p, p, lam, lam_init, diff_attn_prompt, gla_zero, prompt_segments,
                                lru_zero, buf_zero)
        attend_s = functools.partial(diff_attn_sample, cache_k=cache_k, cache_v=cache_v,
                                     page_table=page_table, layer=l)
        h_s, st_s = trunk_layer(h_s, p, lam, lam_init, attend_s, state_gla[l], sample_segments,
                                state_lru_h[l], state_lru_conv[l])
        kp.append(st_p[0]); vp.append(st_p[1]); gp.append(st_p[2]); hp.append(st_p[3]); cp.append(st_p[4])
        ksm.append(st_s[0]); vsm.append(st_s[1]); gs.append(st_s[2]); hs.append(st_s[3]); cs.append(st_s[4])
    y_prompt = h_p[:, N_META:]
    return (y_prompt, h_s, jnp.stack(kp), jnp.stack(vp), jnp.stack(ksm), jnp.stack(vsm),
            jnp.stack(gp), jnp.stack(gs), jnp.stack(hp), jnp.stack(hs), jnp.stack(cp), jnp.stack(cs))
```

```python
import functools
import math

import jax
import jax.numpy as jnp
from jax import lax
from jax.experimental import pallas as pl
from jax.experimental.pallas import tpu as pltpu

F32 = jnp.float32
BF16 = jnp.bfloat16
EPS = 1e-6
GLA_TAU = 16.0
LRU_C = 8.0
NEG = -1e30
LANES = 128
SUBLANES = 8
VMEM_LIMIT_BYTES = 56 * 1024 * 1024

_NT = (((1,), (1,)), ((), ()))


def _tile(n, target, mult):
    best = None
    for d in range(mult, min(n, target) + 1, mult):
        if n % d == 0:
            best = d
    return n if best is None else best


def _params(sem):
    return pltpu.CompilerParams(dimension_semantics=sem, vmem_limit_bytes=VMEM_LIMIT_BYTES)


def _rms(x, g):
    return x * lax.rsqrt(jnp.mean(x * x, axis=-1, keepdims=True) + EPS) * g


def _rms_cast_kernel(x_ref, g_ref, o_ref):
    o_ref[...] = _rms(x_ref[...], g_ref[...]).astype(o_ref.dtype)


def rms_cast(x, g):
    R, D = x.shape
    tr = _tile(R, 416, 16)
    return pl.pallas_call(
        _rms_cast_kernel,
        out_shape=jax.ShapeDtypeStruct((R, D), BF16),
        grid=(R // tr,),
        in_specs=[pl.BlockSpec((tr, D), lambda i: (i, 0)), pl.BlockSpec((1, D), lambda i: (0, 0))],
        out_specs=pl.BlockSpec((tr, D), lambda i: (i, 0)),
        compiler_params=_params(("parallel",)),
        name="rms_cast",
    )(x, g)


def _resid_kernel(h_ref, m_ref, gpost_ref, o_ref):
    o_ref[...] = h_ref[...] + _rms(m_ref[...], gpost_ref[...])


def _resid_next_kernel(h_ref, m_ref, gpost_ref, gpre_ref, o_ref, hn_ref):
    h = h_ref[...] + _rms(m_ref[...], gpost_ref[...])
    o_ref[...] = h
    hn_ref[...] = _rms(h, gpre_ref[...]).astype(hn_ref.dtype)


def resid_norm(h, m, gpost, gpre=None):
    R, D = h.shape
    tr = _tile(R, 208, 16)
    row = pl.BlockSpec((tr, D), lambda i: (i, 0))
    vec = pl.BlockSpec((1, D), lambda i: (0, 0))
    if gpre is None:
        return pl.pallas_call(
            _resid_kernel, out_shape=jax.ShapeDtypeStruct((R, D), F32), grid=(R // tr,),
            in_specs=[row, row, vec], out_specs=row, compiler_params=_params(("parallel",)),
            name="resid_norm")(h, m, gpost), None
    return pl.pallas_call(
        _resid_next_kernel,
        out_shape=(jax.ShapeDtypeStruct((R, D), F32), jax.ShapeDtypeStruct((R, D), BF16)),
        grid=(R // tr,), in_specs=[row, row, vec, vec], out_specs=(row, row),
        compiler_params=_params(("parallel",)), name="resid_norm_next")(h, m, gpost, gpre)


def _mm_kernel(x_ref, w_ref, o_ref, *, relu2):
    acc = jnp.dot(x_ref[...], w_ref[...].astype(BF16), preferred_element_type=F32)
    if relu2:
        acc = jnp.square(jnp.maximum(acc, 0.0))
    o_ref[...] = acc.astype(o_ref.dtype)


def matmul(x, w, layer, n_out, out_dtype, relu2=False, tm_target=1040, tn_target=512):
    R, K = x.shape
    tm = _tile(R, tm_target, 16)
    tn = _tile(n_out, tn_target, LANES)
    return pl.pallas_call(
        functools.partial(_mm_kernel, relu2=relu2),
        out_shape=jax.ShapeDtypeStruct((R, n_out), out_dtype),
        grid=(R // tm, n_out // tn),
        in_specs=[pl.BlockSpec((tm, K), lambda i, j: (i, 0)),
                  pl.BlockSpec((None, K, tn), lambda i, j: (layer, 0, j))],
        out_specs=pl.BlockSpec((tm, tn), lambda i, j: (i, j)),
        compiler_params=_params(("parallel", "parallel")),
        name="matmul",
    )(x, w)


def _mmk_kernel(x_ref, w_ref, o_ref):
    @pl.when(pl.program_id(2) == 0)
    def _():
        o_ref[...] = jnp.zeros_like(o_ref)

    o_ref[...] += jnp.dot(x_ref[...], w_ref[...].astype(BF16), preferred_element_type=F32)


def matmul_ktiled(x, w, layer, tm_target=2080, tn_target=1024, tk_target=1024):
    R, K = x.shape
    N = w.shape[2]
    tm = _tile(R, tm_target, 16)
    tn = _tile(N, tn_target, LANES)
    tk = _tile(K, tk_target, LANES)
    return pl.pallas_call(
        _mmk_kernel,
        out_shape=jax.ShapeDtypeStruct((R, N), F32),
        grid=(R // tm, N // tn, K // tk),
        in_specs=[pl.BlockSpec((tm, tk), lambda i, j, k: (i, k)),
                  pl.BlockSpec((None, tk, tn), lambda i, j, k: (layer, k, j))],
        out_specs=pl.BlockSpec((tm, tn), lambda i, j, k: (i, j)),
        compiler_params=_params(("parallel", "parallel", "arbitrary")),
        name="matmul_ktiled",
    )(x, w)


def _lam_value(lamp_ref, lam_init):
    lp = lamp_ref[...]
    a = jnp.exp(jnp.sum(lp[0:1] * lp[1:2], axis=-1, keepdims=True))
    b = jnp.exp(jnp.sum(lp[2:3] * lp[3:4], axis=-1, keepdims=True))
    return a - b + lam_init


def _attn_prompt_kernel(slopes_ref, lamp_ref, subln_ref, q_ref, k_ref, v_ref, o_ref,
                        kb_ref, vb_ref, s_ref, *, nm, qb, nblk, dh, lam_init):
    T = q_ref.shape[0]
    pad = LANES - nm
    slope = slopes_ref[pl.program_id(1)]
    lam = _lam_value(lamp_ref, lam_init)
    scale = dh ** -0.5

    kb_ref[0:pad, :] = jnp.zeros((pad, 2 * dh), BF16)
    vb_ref[0:pad, :] = jnp.zeros((pad, 2 * dh), BF16)
    kb_ref[pad:pad + T, :] = k_ref[...].astype(BF16)
    vb_ref[pad:pad + T, :] = v_ref[...].astype(BF16)

    lane = lax.broadcasted_iota(jnp.int32, (1, 2 * dh), 1)
    first_map = lane < dh

    def stacked_q(q):
        qs = q * scale
        return jnp.concatenate([jnp.where(first_map, qs, 0.0), jnp.where(first_map, 0.0, qs)],
                               axis=0).astype(BF16)

    def finish(o, l, n):
        o = o / l
        res = o[:n] - lam * o[n:]
        return (_rms(res, subln_ref[...]) * (1.0 - lam_init)).astype(o_ref.dtype)

    qq = stacked_q(q_ref[0:nm, :])
    s = lax.dot_general(qq, kb_ref[0:LANES, :], _NT, preferred_element_type=F32)
    col = lax.broadcasted_iota(jnp.int32, (2 * nm, LANES), 1)
    row = lax.broadcasted_iota(jnp.int32, (2 * nm, LANES), 0)
    row = jnp.where(row >= nm, row - nm, row)
    kpos = col - pad
    s = jnp.where((kpos >= 0) & (kpos <= row), s + slope * kpos.astype(F32), NEG)
    m = jnp.max(s, axis=-1, keepdims=True)
    p = jnp.exp(s - m)
    l = jnp.sum(p, axis=-1, keepdims=True)
    o = jnp.dot(p.astype(BF16), vb_ref[0:LANES, :], preferred_element_type=F32)
    o_ref[0:nm, :] = finish(o, l, nm)

    c0 = lax.broadcasted_iota(jnp.int32, (1, LANES), 1)
    rl = lax.broadcasted_iota(jnp.int32, (2 * qb, qb), 0)
    rl = jnp.where(rl >= qb, rl - qb, rl)
    causal = lax.broadcasted_iota(jnp.int32, (2 * qb, qb), 1) <= rl

    for qj in range(nblk):
        r0 = nm + qj * qb
        nk = LANES + (qj + 1) * qb
        qq = stacked_q(q_ref[r0:r0 + qb, :])
        s = lax.dot_general(qq, kb_ref[0:nk, :], _NT, preferred_element_type=F32)
        kpos = lax.broadcasted_iota(jnp.int32, (1, nk), 1) - (pad + r0)
        s = s + slope * kpos.astype(F32)
        parts = [jnp.where(c0 >= pad, s[:, 0:LANES], NEG)]
        if nk - qb > LANES:
            parts.append(s[:, LANES:nk - qb])
        parts.append(jnp.where(causal, s[:, nk - qb:nk], NEG))
        s_ref[:, 0:nk] = jnp.concatenate(parts, axis=1)
        m = jnp.max(s_ref[:, 0:nk], axis=-1, keepdims=True)
        p = jnp.exp(s_ref[:, 0:nk] - m)
        l = jnp.sum(p, axis=-1, keepdims=True)
        o = jnp.dot(p.astype(BF16), vb_ref[0:nk, :], preferred_element_type=F32)
        o_ref[r0:r0 + qb, :] = finish(o, l, qb)


def attn_prompt(z, slopes, lamp, subln, *, B, T, nm, H, dh, lam_init):
    seq = T - nm
    qb = _tile(seq, 256, LANES)
    nblk = seq // qb
    dv = 2 * dh
    kern = functools.partial(_attn_prompt_kernel, nm=nm, qb=qb, nblk=nblk, dh=dh, lam_init=lam_init)
    return pl.pallas_call(
        kern,
        out_shape=jax.ShapeDtypeStruct((B * T, H * dv), BF16),
        grid=(B, H),
        in_specs=[pl.BlockSpec(memory_space=pltpu.SMEM),
                  pl.BlockSpec((4, dh), lambda b, h: (0, 0)),
                  pl.BlockSpec((1, dv), lambda b, h: (0, 0)),
                  pl.BlockSpec((T, dv), lambda b, h: (b, h)),
                  pl.BlockSpec((T, dv), lambda b, h: (b, H + h)),
                  pl.BlockSpec((T, dv), lambda b, h: (b, 2 * H + h))],
        out_specs=pl.BlockSpec((T, dv), lambda b, h: (b, h)),
        scratch_shapes=[pltpu.VMEM((LANES + seq, dv), BF16), pltpu.VMEM((LANES + seq, dv), BF16),
                        pltpu.VMEM((2 * qb, LANES + seq), F32)],
        compiler_params=_params(("parallel", "parallel")),
        name="attn_prompt",
    )(slopes, lamp, subln, z, z, z)


def _attn_sample_kernel(pt_ref, lamp_ref, subln_ref, slope_ref, q_ref, kn_ref, vn_ref, *rest,
                        G, page, past_len, H, S, dv, lam_init):
    k_refs, v_refs = rest[:G], rest[G:2 * G]
    o_ref, m_ref, l_ref, acc_ref = rest[2 * G:]
    j = pl.program_id(1)
    nrow = H * 2 * S
    h_shift, s_shift = H.bit_length() - 1, (2 * S).bit_length() - 1
    row = lax.broadcasted_iota(jnp.int32, (nrow, 1), 0)
    row_head = lax.shift_right_logical(row, s_shift)
    slope = slope_ref[...]

    @pl.when(j == 0)
    def _():
        m_ref[...] = jnp.full(m_ref.shape, NEG, F32)
        l_ref[...] = jnp.zeros(l_ref.shape, F32)
        acc_ref[...] = jnp.zeros(acc_ref.shape, F32)

    def update(kk, vv, key0, limit):
        n = kk.shape[0]
        s = lax.dot_general(q_ref[...], kk, _NT, preferred_element_type=F32)
        col = lax.broadcasted_iota(jnp.int32, (1, n), 1)
        rel = key0 + lax.shift_right_logical(col, h_shift)
        ok = (col & (H - 1)) == row_head
        if limit is not None:
            ok = ok & (rel <= limit)
        s = jnp.where(ok, s + slope * rel.astype(F32), NEG)
        m_old = m_ref[...]
        m_new = jnp.maximum(m_old, jnp.max(s, axis=-1, keepdims=True))
        alpha = jnp.exp(m_old - m_new)
        p = jnp.exp(s - m_new)
        l_ref[...] = alpha * l_ref[...] + jnp.sum(p, axis=-1, keepdims=True)
        acc_ref[...] = alpha * acc_ref[...] + jnp.dot(p.astype(BF16), vv, preferred_element_type=F32)
        m_ref[...] = m_new

    kk = jnp.concatenate([r[...].astype(BF16) for r in k_refs], axis=0)
    vv = jnp.concatenate([r[...].astype(BF16) for r in v_refs], axis=0)
    update(kk, vv, j * (G * page) - past_len, None)

    @pl.when(j == pl.num_programs(1) - 1)
    def _():
        update(kn_ref[...].astype(BF16), vn_ref[...].astype(BF16), 0, row & (S - 1))
        lam = _lam_value(lamp_ref, lam_init)
        o = acc_ref[...] / l_ref[...]
        for h in range(H):
            blk = o[h * 2 * S:(h + 1) * 2 * S, :]
            res = blk[:S] - lam * blk[S:]
            o_ref[:, h * dv:(h + 1) * dv] = _rms(res, subln_ref[...]) * (1.0 - lam_init)


def attn_sample(qrows, knew, vnew, slope_col, lamp, subln, cache_k, cache_v, page_table, layer, *,
                DB, S, H, dh, lam_init):
    dv = 2 * dh
    n_pages = page_table.shape[1]
    page = cache_k.shape[2]
    assert H & (H - 1) == 0 and S & (S - 1) == 0
    G = _tile(n_pages, 8, 1)
    ck = cache_k.reshape(cache_k.shape[0], cache_k.shape[1], page * H, dv)
    cv = cache_v.reshape(cache_v.shape[0], cache_v.shape[1], page * H, dv)
    nrow = H * 2 * S

    def page_spec(g):
        return pl.BlockSpec((None, None, page * H, dv), lambda b, j, pt: (layer, pt[b, j * G + g], 0, 0))

    kern = functools.partial(_attn_sample_kernel, G=G, page=page, past_len=n_pages * page,
                             H=H, S=S, dv=dv, lam_init=lam_init)
    per_b = lambda rows: pl.BlockSpec((None, rows, dv), lambda b, j, pt: (b, 0, 0))
    grid_spec = pltpu.PrefetchScalarGridSpec(
        num_scalar_prefetch=1,
        grid=(DB, n_pages // G),
        in_specs=[pl.BlockSpec((4, dh), lambda b, j, pt: (0, 0)),
                  pl.BlockSpec((1, dv), lambda b, j, pt: (0, 0)),
                  pl.BlockSpec((nrow, 1), lambda b, j, pt: (0, 0)),
                  per_b(nrow), per_b(knew.shape[1]), per_b(vnew.shape[1])]
                 + [page_spec(g) for g in range(G)] + [page_spec(g) for g in range(G)],
        out_specs=pl.BlockSpec((S, H * dv), lambda b, j, pt: (b, 0)),
        scratch_shapes=[pltpu.VMEM((nrow, 1), F32), pltpu.VMEM((nrow, 1), F32),
                        pltpu.VMEM((nrow, dv), F32)],
    )
    return pl.pallas_call(
        kern, out_shape=jax.ShapeDtypeStruct((DB * S, H * dv), F32), grid_spec=grid_spec,
        compiler_params=_params(("parallel", "arbitrary")), name="attn_sample",
    )(page_table, lamp, subln, slope_col, qrows, knew, vnew, *([ck] * G), *([cv] * G))


def _gla_kernel(q_ref, k_ref, v_ref, g_ref, r_ref, wg_ref, bg_ref, norm_ref, s0_ref, o_ref, sout_ref,
                *, blocks, dk, dvb):
    L = LANES
    ri = lax.broadcasted_iota(jnp.int32, (L, L), 0)
    ci = lax.broadcasted_iota(jnp.int32, (L, L), 1)
    tril = ri >= ci
    tril_bf = jnp.where(tril, 1.0, 0.0).astype(BF16)
    rowi = lax.broadcasted_iota(jnp.int32, (L, 1), 0)
    wg = wg_ref[...].astype(BF16)
    S = s0_ref[...]
    qscale = dk ** -0.5

    def load(ref, row0, nreal, off):
        x = ref[row0:row0 + nreal, :]
        if nreal == L:
            return x
        parts = []
        if off:
            parts.append(jnp.zeros((off, x.shape[1]), x.dtype))
        parts.append(x)
        if L - off - nreal:
            parts.append(jnp.zeros((L - off - nreal, x.shape[1]), x.dtype))
        return jnp.concatenate(parts, axis=0)

    for row0, nreal, off in blocks:
        q = load(q_ref, row0, nreal, off) * qscale
        k = load(k_ref, row0, nreal, off)
        v = load(v_ref, row0, nreal, off).astype(BF16)
        r = load(r_ref, row0, nreal, off).astype(BF16)
        x = jnp.dot(r, wg, preferred_element_type=F32) + bg_ref[...]
        gk = (jnp.minimum(x, 0.0) - jnp.log1p(jnp.exp(-jnp.abs(x)))) * (1.0 / GLA_TAU)
        if nreal < L:
            gk = jnp.where((rowi >= off) & (rowi < off + nreal), gk, 0.0)
        hi = gk.astype(BF16)
        lo = (gk - hi.astype(F32)).astype(BF16)
        b = (jnp.dot(tril_bf, hi, preferred_element_type=F32)
             + jnp.dot(tril_bf, lo, preferred_element_type=F32))
        b_last = b[L - 1:L, :]
        b_mid = b[L // 2 - 1:L // 2, :]
        qs = (q * jnp.exp(b)).astype(BF16)
        qm = (q * jnp.exp(b - b_mid)).astype(BF16)
        km = (k * jnp.exp(b_mid - b)).astype(BF16)
        att = lax.dot_general(qm, km, _NT, preferred_element_type=F32)
        att = jnp.where(tril, att, 0.0).astype(BF16)
        o = (jnp.dot(att, v, preferred_element_type=F32)
             + jnp.dot(qs, S.astype(BF16), preferred_element_type=F32))
        kk = (k * jnp.exp(b_last - b)).T.astype(BF16)
        decay = jnp.broadcast_to(jnp.exp(b_last), (L, dk)).T
        S = jnp.concatenate([decay] * (dvb // L), axis=1) * S + jnp.dot(kk, v, preferred_element_type=F32)
        y = _rms(o, norm_ref[...])
        g = load(g_ref, row0, nreal, off)
        y = y * (g / (1.0 + jnp.exp(-g)))
        o_ref[row0:row0 + nreal, :] = y[off:off + nreal, :].astype(o_ref.dtype)
    sout_ref[...] = S


def gla(z, zr, wg, bgate, norm, s0, layer, *, row0, NB, T, H, dk, dvb, cols, blocks, out_dtype):
    cq, ck_, cv_, cg_ = cols
    rb = row0 // T
    kern = functools.partial(_gla_kernel, blocks=blocks, dk=dk, dvb=dvb)
    return pl.pallas_call(
        kern,
        out_shape=(jax.ShapeDtypeStruct((NB * T, H * dvb), out_dtype),
                   jax.ShapeDtypeStruct((NB, H, dk, dvb), F32)),
        grid=(NB, H),
        in_specs=[pl.BlockSpec((T, dk), lambda b, h: (rb + b, cq + h)),
                  pl.BlockSpec((T, dk), lambda b, h: (rb + b, ck_ + h)),
                  pl.BlockSpec((T, dvb), lambda b, h: (rb + b, cv_ + h)),
                  pl.BlockSpec((T, dvb), lambda b, h: (rb + b, cg_ + h)),
                  pl.BlockSpec((T, LANES), lambda b, h: (rb + b, 0)),
                  pl.BlockSpec((None, LANES, dk), lambda b, h: (layer, 0, h)),
                  pl.BlockSpec((None, 1, dk), lambda b, h: (layer, 0, h)),
                  pl.BlockSpec((None, 1, dvb), lambda b, h: (layer, 0, 0)),
                  pl.BlockSpec((None, None, dk, dvb), lambda b, h: (b, h, 0, 0))],
        out_specs=(pl.BlockSpec((T, dvb), lambda b, h: (b, h)),
                   pl.BlockSpec((None, None, dk, dvb), lambda b, h: (b, h, 0, 0))),
        compiler_params=_params(("parallel", "parallel")),
        name="gla",
    )(z, z, z, z, zr, wg, bgate, norm, s0)


def _lru_kernel(cx_ref, cg_ref, buf_ref, h0_ref, cw_ref, cb_ref, wa_ref, ba_ref, wx_ref, bx_ref, lam_ref,
                y_ref, hl_ref, nb_ref, e_ref, a_ref, x_ref, hs_ref, *, T, cw, bw, conv_w):
    e_ref[0:SUBLANES, :] = buf_ref[...]
    e_ref[SUBLANES:SUBLANES + T, :] = cx_ref[...]
    first = SUBLANES - (conv_w - 1)
    u = cb_ref[...] + cw_ref[0:1, :] * e_ref[first:first + T, :]
    for j in range(1, conv_w):
        u = u + cw_ref[j:j + 1, :] * e_ref[first + j:first + j + T, :]
    nb_ref[...] = e_ref[T:T + SUBLANES, :]

    lam = lam_ref[...]
    sp = jnp.maximum(-lam, 0.0) + jnp.log1p(jnp.exp(-jnp.abs(lam)))
    for n in range(cw // bw):
        sl = slice(n * bw, (n + 1) * bw)
        un = u[:, sl]
        ub = un.astype(BF16)
        r = jax.nn.sigmoid(jnp.dot(ub, wa_ref[n].astype(BF16), preferred_element_type=F32) + ba_ref[:, sl])
        i = jax.nn.sigmoid(jnp.dot(ub, wx_ref[n].astype(BF16), preferred_element_type=F32) + bx_ref[:, sl])
        log_a = (-LRU_C) * r * sp[:, sl]
        a = jnp.exp(log_a)
        a_ref[:, sl] = a
        x_ref[:, sl] = jnp.sqrt((1.0 - a) * (1.0 + a)) * (i * un)

    def body(i8, h):
        for s in range(SUBLANES):
            t = i8 * SUBLANES + s
            h = a_ref[pl.ds(t, 1), :] * h + x_ref[pl.ds(t, 1), :]
            hs_ref[pl.ds(t, 1), :] = h
        return h

    h = lax.fori_loop(0, T // SUBLANES, body, h0_ref[...])
    hl_ref[...] = h
    y_ref[...] = (hs_ref[...] * jax.nn.gelu(cg_ref[...], approximate=True)).astype(y_ref.dtype)


def rglru(zc, bufp, h0, conv_w, conv_b, w_a, b_a, w_x, b_x, lam, layer, *, row0, NB, T, d_lru, out_dtype):
    n_blocks, bw = w_a.shape[1], w_a.shape[2]
    cw = _tile(d_lru, 256, bw)
    ncg = d_lru // cw
    nbk = cw // bw
    cwid = conv_w.shape[1]
    rb = row0 // T
    kern = functools.partial(_lru_kernel, T=T, cw=cw, bw=bw, conv_w=cwid)
    vec = lambda rows: pl.BlockSpec((None, rows, cw), lambda b, j: (layer, 0, j))
    return pl.pallas_call(
        kern,
        out_shape=(jax.ShapeDtypeStruct((NB * T, d_lru), out_dtype),
                   jax.ShapeDtypeStruct((NB, 1, d_lru), F32),
                   jax.ShapeDtypeStruct((NB, SUBLANES, d_lru), F32)),
        grid=(NB, ncg),
        in_specs=[pl.BlockSpec((T, cw), lambda b, j: (rb + b, j)),
                  pl.BlockSpec((T, cw), lambda b, j: (rb + b, ncg + j)),
                  pl.BlockSpec((None, SUBLANES, cw), lambda b, j: (b, 0, j)),
                  pl.BlockSpec((None, 1, cw), lambda b, j: (b, 0, j)),
                  vec(cwid), vec(1),
                  pl.BlockSpec((None, nbk, bw, bw), lambda b, j: (layer, j, 0, 0)), vec(1),
                  pl.BlockSpec((None, nbk, bw, bw), lambda b, j: (layer, j, 0, 0)), vec(1), vec(1)],
        out_specs=(pl.BlockSpec((T, cw), lambda b, j: (b, j)),
                   pl.BlockSpec((None, 1, cw), lambda b, j: (b, 0, j)),
                   pl.BlockSpec((None, SUBLANES, cw), lambda b, j: (b, 0, j))),
        scratch_shapes=[pltpu.VMEM((T + SUBLANES, cw), F32), pltpu.VMEM((T, cw), F32),
                        pltpu.VMEM((T, cw), F32), pltpu.VMEM((T, cw), F32)],
        compiler_params=_params(("parallel", "parallel")),
        name="rglru",
    )(zc, zc, bufp, h0, conv_w, conv_b, w_a, b_a, w_x, b_x, lam)


def kernel(x_prompt, x_sample, cache_k, cache_v, state_gla, state_lru_h, state_lru_conv, page_table, meta_tokens, norm_mix_pre, norm_mix_post, norm_mlp_pre, norm_mlp_post, w_in, w_out, lam_q1, lam_k1, lam_q2, lam_k2, attn_subln, gla_w_gate, gla_b_gate, gla_norm, lru_conv_w, lru_conv_b, lru_w_a, lru_b_a, lru_w_x, lru_b_x, lru_lambda, mlp_w_up, mlp_w_down):
    B, seq, D = x_prompt.shape
    DB, S, _ = x_sample.shape
    nm = meta_tokens.shape[0]
    T = nm + seq
    depth = w_in.shape[0]
    H_A, dv_a = cache_k.shape[3], cache_v.shape[4]
    dh_a = dv_a // 2
    d_a = H_A * dv_a
    H_B, dk_b, dv_b = state_gla.shape[2:]
    d_b = H_B * dv_b
    rank = gla_w_gate.shape[1]
    d_lru = state_lru_h.shape[-1]
    conv_w = lru_conv_w.shape[1]
    d_ff = mlp_w_up.shape[2]
    R_P, R_S = B * T, DB * S
    assert nm <= LANES and nm % 16 == 0 and seq % LANES == 0 and S <= SUBLANES and R_P % S == 0
    assert dv_a == LANES and dk_b == LANES and dv_b % LANES == 0 and rank <= LANES

    c_ak, c_av = d_a, 2 * d_a
    c_bq = 3 * d_a
    c_bk = c_bq + H_B * dk_b
    c_bv = c_bk + H_B * dk_b
    c_bg = c_bv + d_b
    c_br = c_bg + d_b
    c_cx = c_br + rank
    n_ab = c_br
    gla_cols = (c_bq // dk_b, c_bk // dk_b, c_bv // dv_b, c_bg // dv_b)

    w_c = w_in[:, :, c_cx:]
    w_r = jnp.pad(w_in[:, :, c_br:c_br + rank], ((0, 0), (0, 0), (0, LANES - rank)))
    wg = jnp.pad(gla_w_gate, ((0, 0), (0, LANES - rank), (0, 0)))
    row3 = lambda a: a.reshape(depth, 1, a.shape[-1])

    slopes = jnp.exp2(-8.0 * jnp.arange(1, H_A + 1, dtype=F32) / H_A)
    slope_col = jnp.repeat(slopes, 2 * S).reshape(H_A * 2 * S, 1)
    eye_c = jnp.eye(2, dtype=F32)
    assert S * H_A <= LANES

    blocks_p = ((0, nm, LANES - nm),) + tuple((nm + LANES * n, LANES, 0) for n in range(seq // LANES))
    blocks_s = ((0, S, 0),)

    h = jnp.concatenate([
        jnp.concatenate([jnp.broadcast_to(meta_tokens[None], (B, nm, D)), x_prompt], axis=1).reshape(R_P, D),
        x_sample.reshape(R_S, D)], axis=0)
    zeros_gla = jnp.zeros((B, H_B, dk_b, dv_b), F32)
    zeros_h = jnp.zeros((B, 1, d_lru), F32)
    zeros_buf = jnp.zeros((B, SUBLANES, d_lru), F32)

    outs = [[] for _ in range(10)]
    hn = rms_cast(h, norm_mix_pre[0:1])
    for l in range(depth):
        lam_init = 0.8 - 0.6 * math.exp(-0.3 * l)
        lamp = jnp.stack([lam_q1[l], lam_k1[l], lam_q2[l], lam_k2[l]])
        subln = attn_subln[l:l + 1]

        z = matmul(hn, w_in, l, n_ab, F32)
        zc = matmul(hn, w_c, l, 2 * d_lru, F32)
        zr = matmul(hn, w_r, l, LANES, F32)

        oa_p = attn_prompt(z, slopes, lamp, subln, B=B, T=T, nm=nm, H=H_A, dh=dh_a, lam_init=lam_init)
        q_s = z[R_P:, :d_a].reshape(DB, S, H_A, 2, dh_a) * (dh_a ** -0.5)
        qrows = jnp.einsum('bihcd,cy->bhciyd', q_s, eye_c).reshape(DB, H_A * 2 * S, dv_a).astype(BF16)
        new_rows = lambda c0: jnp.pad(z[R_P:, c0:c0 + d_a].reshape(DB, S * H_A, dv_a),
                                      ((0, 0), (0, LANES - S * H_A), (0, 0)))
        oa_s = attn_sample(qrows, new_rows(c_ak), new_rows(c_av), slope_col, lamp, subln, cache_k, cache_v,
                           page_table, l, DB=DB, S=S, H=H_A, dh=dh_a, lam_init=lam_init)
        gla_args = (wg, row3(gla_b_gate), row3(gla_norm))
        ob_p, sg_p = gla(z, zr, *gla_args, zeros_gla, l, row0=0, NB=B, T=T, H=H_B, dk=dk_b, dvb=dv_b,
                         cols=gla_cols, blocks=blocks_p, out_dtype=BF16)
        ob_s, sg_s = gla(z, zr, *gla_args, state_gla[l], l, row0=R_P, NB=DB, T=S, H=H_B, dk=dk_b, dvb=dv_b,
                         cols=gla_cols, blocks=blocks_s, out_dtype=F32)
        lru_args = (lru_conv_w, row3(lru_conv_b), lru_w_a, row3(lru_b_a), lru_w_x, row3(lru_b_x),
                    row3(lru_lambda))
        oc_p, hl_p, nb_p = rglru(zc, zeros_buf, zeros_h, *lru_args, l, row0=0, NB=B, T=T, d_lru=d_lru,
                                 out_dtype=BF16)
        bufp = jnp.pad(state_lru_conv[l], ((0, 0), (SUBLANES - (conv_w - 1), 0), (0, 0)))
        oc_s, hl_s, nb_s = rglru(zc, bufp, state_lru_h[l][:, None, :], *lru_args, l, row0=R_P, NB=DB, T=S,
                                 d_lru=d_lru, out_dtype=F32)

        mix = jnp.concatenate([jnp.concatenate([oa_p, ob_p, oc_p], axis=1),
                               jnp.concatenate([oa_s, ob_s, oc_s], axis=1).astype(BF16)], axis=0)
        m1 = matmul(mix, w_out, l, D, F32)
        h, hn2 = resid_norm(h, m1, norm_mix_post[l:l + 1], norm_mlp_pre[l:l + 1])
        up = matmul(hn2, mlp_w_up, l, d_ff, BF16, relu2=True)
        dn = matmul_ktiled(up, mlp_w_down, l)
        h, hn = resid_norm(h, dn, norm_mlp_post[l:l + 1],
                           norm_mix_pre[l + 1:l + 2] if l + 1 < depth else None)

        new = (z[:R_P, c_ak:c_av].reshape(B, T, H_A, dv_a), z[:R_P, c_av:c_bq].reshape(B, T, H_A, dv_a),
               z[R_P:, c_ak:c_av].reshape(DB, S, H_A, dv_a), z[R_P:, c_av:c_bq].reshape(DB, S, H_A, dv_a),
               sg_p, sg_s, hl_p[:, 0, :], hl_s[:, 0, :],
               nb_p[:, SUBLANES - (conv_w - 1):, :], nb_s[:, SUBLANES - (conv_w - 1):, :])
        for acc, val in zip(outs, new):
            acc.append(val)

    y_prompt = h[:R_P].reshape(B, T, D)[:, nm:]
    y_sample = h[R_P:].reshape(DB, S, D)
    return (y_prompt, y_sample) + tuple(jnp.stack(o) for o in outs)
```

```python
import functools
import math

import jax
import jax.numpy as jnp
from jax import lax
from jax.experimental import pallas as pl
from jax.experimental.pallas import tpu as pltpu

F32 = jnp.float32
BF16 = jnp.bfloat16
EPS = 1e-6
GLA_TAU = 16.0
LRU_C = 8.0
NEG = -1e30
LANES = 128
SUBLANES = 8
VMEM_LIMIT_BYTES = 56 * 1024 * 1024

_NT = (((1,), (1,)), ((), ()))


def _tile(n, target, mult):
    best = None
    for d in range(mult, min(n, target) + 1, mult):
        if n % d == 0:
            best = d
    return n if best is None else best


def _params(sem):
    return pltpu.CompilerParams(dimension_semantics=sem, vmem_limit_bytes=VMEM_LIMIT_BYTES)


def _rms(x, g):
    return x * lax.rsqrt(jnp.mean(x * x, axis=-1, keepdims=True) + EPS) * g


def _rms_cast_kernel(x_ref, g_ref, o_ref):
    o_ref[...] = _rms(x_ref[...], g_ref[...]).astype(o_ref.dtype)


def rms_cast(x, g):
    R, D = x.shape
    tr = _tile(R, 416, 16)
    return pl.pallas_call(
        _rms_cast_kernel,
        out_shape=jax.ShapeDtypeStruct((R, D), BF16),
        grid=(R // tr,),
        in_specs=[pl.BlockSpec((tr, D), lambda i: (i, 0)), pl.BlockSpec((1, D), lambda i: (0, 0))],
        out_specs=pl.BlockSpec((tr, D), lambda i: (i, 0)),
        compiler_params=_params(("parallel",)),
        name="rms_cast",
    )(x, g)


def _resid_kernel(h_ref, m_ref, gpost_ref, o_ref):
    o_ref[...] = h_ref[...] + _rms(m_ref[...], gpost_ref[...])


def _resid_next_kernel(h_ref, m_ref, gpost_ref, gpre_ref, o_ref, hn_ref):
    h = h_ref[...] + _rms(m_ref[...], gpost_ref[...])
    o_ref[...] = h
    hn_ref[...] = _rms(h, gpre_ref[...]).astype(hn_ref.dtype)


def resid_norm(h, m, gpost, gpre=None):
    R, D = h.shape
    tr = _tile(R, 208, 16)
    row = pl.BlockSpec((tr, D), lambda i: (i, 0))
    vec = pl.BlockSpec((1, D), lambda i: (0, 0))
    if gpre is None:
        return pl.pallas_call(
            _resid_kernel, out_shape=jax.ShapeDtypeStruct((R, D), F32), grid=(R // tr,),
            in_specs=[row, row, vec], out_specs=row, compiler_params=_params(("parallel",)),
            name="resid_norm")(h, m, gpost), None
    return pl.pallas_call(
        _resid_next_kernel,
        out_shape=(jax.ShapeDtypeStruct((R, D), F32), jax.ShapeDtypeStruct((R, D), BF16)),
        grid=(R // tr,), in_specs=[row, row, vec, vec], out_specs=(row, row),
        compiler_params=_params(("parallel",)), name="resid_norm_next")(h, m, gpost, gpre)


def resid_prompt_out(h, m, gpost, *, B, T, nm):
    D = h.shape[1]
    seq = T - nm
    tr = _tile(seq, 256, SUBLANES)
    rows = pl.BlockSpec((pl.Element(tr), pl.Element(D)),
                        lambda b, i: (pl.multiple_of(b * T + nm + i * tr, SUBLANES), 0))
    return pl.pallas_call(
        _resid_kernel, out_shape=jax.ShapeDtypeStruct((B, seq, D), F32), grid=(B, seq // tr),
        in_specs=[rows, rows, pl.BlockSpec((1, D), lambda b, i: (0, 0))],
        out_specs=pl.BlockSpec((None, tr, D), lambda b, i: (b, i, 0)),
        compiler_params=_params(("parallel", "parallel")), name="resid_prompt_out")(h, m, gpost)


def _kv_export_kernel(k_ref, v_ref, *rest, H, dv):
    ko_ref, vo_ref = rest[-2:]
    tt = k_ref.shape[0]
    for h in range(H):
        ko_ref[pl.ds(h, tt, stride=H), :] = k_ref[:, h * dv:(h + 1) * dv]
        vo_ref[pl.ds(h, tt, stride=H), :] = v_ref[:, h * dv:(h + 1) * dv]


def kv_export(z, prev, layer, *, depth, B, T, H, dv):
    tt = _tile(T, 344, SUBLANES)
    nt = T // tt
    out = jax.ShapeDtypeStruct((depth * B * T * H, dv), F32)
    in_specs = [pl.BlockSpec((tt, H * dv), lambda b, i: (b * nt + i, 1)),
                pl.BlockSpec((tt, H * dv), lambda b, i: (b * nt + i, 2))]
    args, aliases = [z, z], {}
    if prev is not None:
        in_specs += [pl.BlockSpec(memory_space=pl.ANY)] * 2
        args += list(prev)
        aliases = {2: 0, 3: 1}
    out_spec = pl.BlockSpec((tt * H, dv), lambda b, i: ((layer * B + b) * nt + i, 0))
    return pl.pallas_call(
        functools.partial(_kv_export_kernel, H=H, dv=dv), out_shape=(out, out), grid=(B, nt),
        in_specs=in_specs, out_specs=(out_spec, out_spec), input_output_aliases=aliases,
        compiler_params=_params(("parallel", "parallel")), name="kv_export")(*args)


def _mm_kernel(x_ref, w_ref, o_ref, *, relu2):
    acc = jnp.dot(x_ref[...], w_ref[...].astype(BF16), preferred_element_type=F32)
    if relu2:
        acc = jnp.square(jnp.maximum(acc, 0.0))
    o_ref[...] = acc.astype(o_ref.dtype)


def _x_block(tm, K, tm_target):
    mode = dict(pipeline_mode=pl.Buffered(1)) if tm_target > 1040 else {}
    return pl.BlockSpec((tm, K), lambda i, j: (i, 0), **mode)


def matmul(x, w, layer, n_out, out_dtype, relu2=False, tm_target=1040, tn_target=512):
    R, K = x.shape
    tm = _tile(R, tm_target, 16)
    tn = _tile(n_out, tn_target, LANES)
    return pl.pallas_call(
        functools.partial(_mm_kernel, relu2=relu2),
        out_shape=jax.ShapeDtypeStruct((R, n_out), out_dtype),
        grid=(R // tm, n_out // tn),
        in_specs=[_x_block(tm, K, tm_target),
                  pl.BlockSpec((None, K, tn), lambda i, j: (layer, 0, j))],
        out_specs=pl.BlockSpec((tm, tn), lambda i, j: (i, j)),
        compiler_params=_params(("parallel", "parallel")),
        name="matmul",
    )(x, w)


def _mm_nt_kernel(x_ref, w_ref, o_ref):
    w = w_ref[0].astype(BF16)
    pad = o_ref.shape[1] - w.shape[0]
    if pad:
        w = jnp.concatenate([w, jnp.zeros((pad, w.shape[1]), BF16)], axis=0)
    o_ref[...] = lax.dot_general(x_ref[...], w, _NT, preferred_element_type=F32)


def matmul_nt(x, wt, layer, row0, n_rows, tm_target=1040, tn_target=512):
    R, K = x.shape
    assert row0 % SUBLANES == 0
    tm = _tile(R, tm_target, 16)
    tn = _tile(n_rows, tn_target, LANES) if n_rows >= LANES else n_rows
    n_out = max(n_rows, LANES)
    to = max(tn, LANES)
    return pl.pallas_call(
        _mm_nt_kernel,
        out_shape=jax.ShapeDtypeStruct((R, n_out), F32),
        grid=(R // tm, n_rows // tn),
        in_specs=[_x_block(tm, K, tm_target),
                  pl.BlockSpec((pl.Element(1), pl.Element(tn), pl.Element(K)),
                               lambda i, j: (layer, pl.multiple_of(row0 + j * tn, SUBLANES), 0))],
        out_specs=pl.BlockSpec((tm, to), lambda i, j: (i, j)),
        compiler_params=_params(("parallel", "parallel")),
        name="matmul_nt",
    )(x, wt)


def _mmk_kernel(x_ref, w_ref, o_ref):
    @pl.when(pl.program_id(2) == 0)
    def _():
        o_ref[...] = jnp.zeros_like(o_ref)

    o_ref[...] += jnp.dot(x_ref[...], w_ref[...].astype(BF16), preferred_element_type=F32)


def matmul_ktiled(x, w, layer, tm_target=2080, tn_target=1024, tk_target=1024):
    R, K = x.shape
    N = w.shape[2]
    tm = _tile(R, tm_target, 16)
    tn = _tile(N, tn_target, LANES)
    tk = _tile(K, tk_target, LANES)
    return pl.pallas_call(
        _mmk_kernel,
        out_shape=jax.ShapeDtypeStruct((R, N), F32),
        grid=(R // tm, N // tn, K // tk),
        in_specs=[pl.BlockSpec((tm, tk), lambda i, j, k: (i, k)),
                  pl.BlockSpec((None, tk, tn), lambda i, j, k: (layer, k, j))],
        out_specs=pl.BlockSpec((tm, tn), lambda i, j, k: (i, j)),
        compiler_params=_params(("parallel", "parallel", "arbitrary")),
        name="matmul_ktiled",
    )(x, w)


def _lam_value(lamp_ref, lam_init):
    lp = lamp_ref[...]
    a = jnp.exp(jnp.sum(lp[0:1] * lp[1:2], axis=-1, keepdims=True))
    b = jnp.exp(jnp.sum(lp[2:3] * lp[3:4], axis=-1, keepdims=True))
    return a - b + lam_init


def _attn_prompt_kernel(slopes_ref, lamp_ref, subln_ref, q_ref, k_ref, v_ref, o_ref,
                        kb_ref, vb_ref, s_ref, *, nm, qb, nblk, dh, lam_init):
    T = q_ref.shape[0]
    pad = LANES - nm
    slope = slopes_ref[pl.program_id(1)]
    lam = _lam_value(lamp_ref, lam_init)
    scale = dh ** -0.5

    kb_ref[0:pad, :] = jnp.zeros((pad, 2 * dh), BF16)
    vb_ref[0:pad, :] = jnp.zeros((pad, 2 * dh), BF16)
    kb_ref[pad:pad + T, :] = k_ref[...].astype(BF16)
    vb_ref[pad:pad + T, :] = v_ref[...].astype(BF16)

    lane = lax.broadcasted_iota(jnp.int32, (1, 2 * dh), 1)
    first_map = lane < dh

    def stacked_q(q):
        qs = q * scale
        return jnp.concatenate([jnp.where(first_map, qs, 0.0), jnp.where(first_map, 0.0, qs)],
                               axis=0).astype(BF16)

    def finish(o, l, n):
        o = o / l
        res = o[:n] - lam * o[n:]
        return (_rms(res, subln_ref[...]) * (1.0 - lam_init)).astype(o_ref.dtype)

    qq = stacked_q(q_ref[0:nm, :])
    s = lax.dot_general(qq, kb_ref[0:LANES, :], _NT, preferred_element_type=F32)
    col = lax.broadcasted_iota(jnp.int32, (2 * nm, LANES), 1)
    row = lax.broadcasted_iota(jnp.int32, (2 * nm, LANES), 0)
    row = jnp.where(row >= nm, row - nm, row)
    kpos = col - pad
    s = jnp.where((kpos >= 0) & (kpos <= row), s + slope * kpos.astype(F32), NEG)
    m = jnp.max(s, axis=-1, keepdims=True)
    p = jnp.exp(s - m)
    l = jnp.sum(p, axis=-1, keepdims=True)
    o = jnp.dot(p.astype(BF16), vb_ref[0:LANES, :], preferred_element_type=F32)
    o_ref[0:nm, :] = finish(o, l, nm)

    c0 = lax.broadcasted_iota(jnp.int32, (1, LANES), 1)
    rl = lax.broadcasted_iota(jnp.int32, (2 * qb, qb), 0)
    rl = jnp.where(rl >= qb, rl - qb, rl)
    causal = lax.broadcasted_iota(jnp.int32, (2 * qb, qb), 1) <= rl

    for qj in range(nblk):
        r0 = nm + qj * qb
        nk = LANES + (qj + 1) * qb
        qq = stacked_q(q_ref[r0:r0 + qb, :])
        s = lax.dot_general(qq, kb_ref[0:nk, :], _NT, preferred_element_type=F32)
        kpos = lax.broadcasted_iota(jnp.int32, (1, nk), 1) - (pad + r0)
        s = s + slope * kpos.astype(F32)
        parts = [jnp.where(c0 >= pad, s[:, 0:LANES], NEG)]
        if nk - qb > LANES:
            parts.append(s[:, LANES:nk - qb])
        parts.append(jnp.where(causal, s[:, nk - qb:nk], NEG))
        s_ref[:, 0:nk] = jnp.concatenate(parts, axis=1)
        m = jnp.max(s_ref[:, 0:nk], axis=-1, keepdims=True)
        p = jnp.exp(s_ref[:, 0:nk] - m)
        l = jnp.sum(p, axis=-1, keepdims=True)
        o = jnp.dot(p.astype(BF16), vb_ref[0:nk, :], preferred_element_type=F32)
        o_ref[r0:r0 + qb, :] = finish(o, l, qb)


def attn_prompt(z, slopes, lamp, subln, *, B, T, nm, H, dh, lam_init, mix_shape):
    seq = T - nm
    qb = _tile(seq, 256, LANES)
    nblk = seq // qb
    dv = 2 * dh
    kern = functools.partial(_attn_prompt_kernel, nm=nm, qb=qb, nblk=nblk, dh=dh, lam_init=lam_init)
    return pl.pallas_call(
        kern,
        out_shape=jax.ShapeDtypeStruct(mix_shape, BF16),
        grid=(B, H),
        in_specs=[pl.BlockSpec(memory_space=pltpu.SMEM),
                  pl.BlockSpec((4, dh), lambda b, h: (0, 0)),
                  pl.BlockSpec((1, dv), lambda b, h: (0, 0)),
                  pl.BlockSpec((T, dv), lambda b, h: (b, h)),
                  pl.BlockSpec((T, dv), lambda b, h: (b, H + h)),
                  pl.BlockSpec((T, dv), lambda b, h: (b, 2 * H + h))],
        out_specs=pl.BlockSpec((T, dv), lambda b, h: (b, h)),
        scratch_shapes=[pltpu.VMEM((LANES + seq, dv), BF16), pltpu.VMEM((LANES + seq, dv), BF16),
                        pltpu.VMEM((2 * qb, LANES + seq), F32)],
        compiler_params=_params(("parallel", "parallel")),
        name="attn_prompt",
    )(slopes, lamp, subln, z, z, z)


def _attn_sample_kernel(pt_ref, lamp_ref, subln_ref, slope_ref, q_ref, kn_ref, vn_ref, *rest,
                        G, page, past_len, H, S, dv, lam_init):
    k_refs, v_refs = rest[:G], rest[G:2 * G]
    o_ref, m_ref, l_ref, acc_ref, bias_ref = rest[2 * G:]
    j = pl.program_id(1)
    nrow = H * 2 * S
    h_shift, s_shift = H.bit_length() - 1, (2 * S).bit_length() - 1
    row = lax.broadcasted_iota(jnp.int32, (nrow, 1), 0)
    row_head = lax.shift_right_logical(row, s_shift)
    slope = slope_ref[...]

    def block_bias(n, limit=None):
        col = lax.broadcasted_iota(jnp.int32, (1, n), 1)
        key = lax.shift_right_logical(col, h_shift)
        ok = (col & (H - 1)) == row_head
        if limit is not None:
            ok = ok & (key <= limit)
        return jnp.where(ok, slope * key.astype(F32), NEG)

    @pl.when(j == 0)
    def _():
        m_ref[...] = jnp.full(m_ref.shape, NEG, F32)
        l_ref[...] = jnp.zeros(l_ref.shape, F32)
        acc_ref[...] = jnp.zeros(acc_ref.shape, F32)
        bias_ref[...] = block_bias(bias_ref.shape[1])

    def update(kk, vv, bias, shift):
        s = lax.dot_general(q_ref[...], kk, _NT, preferred_element_type=F32) + bias
        m_old = m_ref[...]
        m_new = jnp.maximum(m_old, jnp.max(s, axis=-1, keepdims=True) + shift)
        alpha = jnp.exp(m_old - m_new)
        p = jnp.exp(s - (m_new - shift))
        l_ref[...] = alpha * l_ref[...] + jnp.sum(p, axis=-1, keepdims=True)
        acc_ref[...] = alpha * acc_ref[...] + jnp.dot(p.astype(BF16), vv, preferred_element_type=F32)
        m_ref[...] = m_new

    kk = jnp.concatenate([r[...].astype(BF16) for r in k_refs], axis=0)
    vv = jnp.concatenate([r[...].astype(BF16) for r in v_refs], axis=0)
    update(kk, vv, bias_ref[...], slope * (j * (G * page) - past_len).astype(F32))

    @pl.when(j == pl.num_programs(1) - 1)
    def _():
        update(kn_ref[...].astype(BF16), vn_ref[...].astype(BF16),
               block_bias(kn_ref.shape[0], row & (S - 1)), 0.0)
        lam = _lam_value(lamp_ref, lam_init)
        o = acc_ref[...] / l_ref[...]
        for h in range(H):
            blk = o[h * 2 * S:(h + 1) * 2 * S, :]
            res = blk[:S] - lam * blk[S:]
            o_ref[:, h * dv:(h + 1) * dv] = _rms(res, subln_ref[...]) * (1.0 - lam_init)


def attn_sample(qrows, knew, vnew, slope_col, lamp, subln, cache_k, cache_v, page_table, layer, *,
                DB, S, H, dh, lam_init):
    dv = 2 * dh
    n_pages = page_table.shape[1]
    page = cache_k.shape[2]
    assert H & (H - 1) == 0 and S & (S - 1) == 0
    G = _tile(n_pages, 8, 1)
    ck = cache_k.reshape(cache_k.shape[0], cache_k.shape[1], page * H, dv)
    cv = cache_v.reshape(cache_v.shape[0], cache_v.shape[1], page * H, dv)
    nrow = H * 2 * S

    def page_spec(g):
        return pl.BlockSpec((None, None, page * H, dv), lambda b, j, pt: (layer, pt[b, j * G + g], 0, 0))

    kern = functools.partial(_attn_sample_kernel, G=G, page=page, past_len=n_pages * page,
                             H=H, S=S, dv=dv, lam_init=lam_init)
    per_b = lambda rows: pl.BlockSpec((None, rows, dv), lambda b, j, pt: (b, 0, 0))
    grid_spec = pltpu.PrefetchScalarGridSpec(
        num_scalar_prefetch=1,
        grid=(DB, n_pages // G),
        in_specs=[pl.BlockSpec((4, dh), lambda b, j, pt: (0, 0)),
                  pl.BlockSpec((1, dv), lambda b, j, pt: (0, 0)),
                  pl.BlockSpec((nrow, 1), lambda b, j, pt: (0, 0)),
                  per_b(nrow), per_b(knew.shape[1]), per_b(vnew.shape[1])]
                 + [page_spec(g) for g in range(G)] + [page_spec(g) for g in range(G)],
        out_specs=pl.BlockSpec((S, H * dv), lambda b, j, pt: (b, 0)),
        scratch_shapes=[pltpu.VMEM((nrow, 1), F32), pltpu.VMEM((nrow, 1), F32),
                        pltpu.VMEM((nrow, dv), F32), pltpu.VMEM((nrow, G * page * H), F32)],
    )
    return pl.pallas_call(
        kern, out_shape=jax.ShapeDtypeStruct((DB * S, H * dv), F32), grid_spec=grid_spec,
        compiler_params=_params(("parallel", "arbitrary")), name="attn_sample",
    )(page_table, lamp, subln, slope_col, qrows, knew, vnew, *([ck] * G), *([cv] * G))


def _gla_kernel(q_ref, k_ref, v_ref, g_ref, r_ref, wg_ref, bg_ref, norm_ref, s0_ref, *rest,
                blocks, dk, dvb, hp):
    o_ref, sout_ref = rest[-2:]
    L = LANES
    ri = lax.broadcasted_iota(jnp.int32, (L, L), 0)
    ci = lax.broadcasted_iota(jnp.int32, (L, L), 1)
    tril = ri >= ci
    tril_bf = jnp.where(tril, 1.0, 0.0).astype(BF16)
    rowi = lax.broadcasted_iota(jnp.int32, (L, 1), 0)
    qscale = dk ** -0.5
    states = [s0_ref[hh] for hh in range(hp)]

    def load(ref, row0, nreal, off, c0, width):
        x = ref[row0:row0 + nreal, c0:c0 + width]
        if nreal == L:
            return x
        parts = []
        if off:
            parts.append(jnp.zeros((off, width), x.dtype))
        parts.append(x)
        if L - off - nreal:
            parts.append(jnp.zeros((L - off - nreal, width), x.dtype))
        return jnp.concatenate(parts, axis=0)

    for row0, nreal, off in blocks:
        r = load(r_ref, row0, nreal, off, 0, L).astype(BF16)
        for hh in range(hp):
            S = states[hh]
            q = load(q_ref, row0, nreal, off, hh * dk, dk) * qscale
            k = load(k_ref, row0, nreal, off, hh * dk, dk)
            v = load(v_ref, row0, nreal, off, hh * dvb, dvb).astype(BF16)
            wg = wg_ref[:, hh * dk:(hh + 1) * dk].astype(BF16)
            x = jnp.dot(r, wg, preferred_element_type=F32) + bg_ref[:, hh * dk:(hh + 1) * dk]
            gk = (jnp.minimum(x, 0.0) - jnp.log1p(jnp.exp(-jnp.abs(x)))) * (1.0 / GLA_TAU)
            if nreal < L:
                gk = jnp.where((rowi >= off) & (rowi < off + nreal), gk, 0.0)
            hi = gk.astype(BF16)
            lo = (gk - hi.astype(F32)).astype(BF16)
            b = (jnp.dot(tril_bf, hi, preferred_element_type=F32)
                 + jnp.dot(tril_bf, lo, preferred_element_type=F32))
            b_last = b[L - 1:L, :]
            b_mid = b[L // 2 - 1:L // 2, :]
            qs = (q * jnp.exp(b)).astype(BF16)
            qm = (q * jnp.exp(b - b_mid)).astype(BF16)
            km = (k * jnp.exp(b_mid - b)).astype(BF16)
            att = lax.dot_general(qm, km, _NT, preferred_element_type=F32)
            att = jnp.where(tril, att, 0.0).astype(BF16)
            o = (jnp.dot(att, v, preferred_element_type=F32)
                 + jnp.dot(qs, S.astype(BF16), preferred_element_type=F32))
            kk = (k * jnp.exp(b_last - b)).T.astype(BF16)
            decay = jnp.broadcast_to(jnp.exp(b_last), (L, dk)).T
            states[hh] = (jnp.concatenate([decay] * (dvb // L), axis=1) * S
                          + jnp.dot(kk, v, preferred_element_type=F32))
            y = _rms(o, norm_ref[...])
            g = load(g_ref, row0, nreal, off, hh * dvb, dvb)
            y = y * (g / (1.0 + jnp.exp(-g)))
            o_ref[row0:row0 + nreal, hh * dvb:(hh + 1) * dvb] = y[off:off + nreal, :].astype(o_ref.dtype)
    for hh in range(hp):
        sout_ref[hh] = states[hh]


def gla(z, zr, wg, bgate, norm, s0, layer, *, row0, NB, T, H, dk, dvb, cols, blocks, mix=None, mix_col0=0):
    hp = 2 if H % 2 == 0 else 1
    cq, ck_, cv_, cg_ = cols
    wq, wv = hp * dk, hp * dvb
    assert cq % wq == 0 and ck_ % wq == 0 and cv_ % wv == 0 and cg_ % wv == 0 and mix_col0 % wv == 0
    rb = row0 // T
    kern = functools.partial(_gla_kernel, blocks=blocks, dk=dk, dvb=dvb, hp=hp)
    in_specs = [pl.BlockSpec((T, wq), lambda b, g: (rb + b, cq // wq + g)),
                pl.BlockSpec((T, wq), lambda b, g: (rb + b, ck_ // wq + g)),
                pl.BlockSpec((T, wv), lambda b, g: (rb + b, cv_ // wv + g)),
                pl.BlockSpec((T, wv), lambda b, g: (rb + b, cg_ // wv + g)),
                pl.BlockSpec((T, LANES), lambda b, g: (rb + b, 0)),
                pl.BlockSpec((None, LANES, wq), lambda b, g: (layer, 0, g)),
                pl.BlockSpec((None, 1, wq), lambda b, g: (layer, 0, g)),
                pl.BlockSpec((None, 1, dvb), lambda b, g: (layer, 0, 0)),
                pl.BlockSpec((None, hp, dk, dvb), lambda b, g: (b, g, 0, 0))]
    args = [z, z, z, z, zr, wg, bgate, norm, s0]
    aliases = {}
    if mix is None:
        y_shape = jax.ShapeDtypeStruct((NB * T, H * dvb), F32)
    else:
        y_shape = jax.ShapeDtypeStruct(mix.shape, mix.dtype)
        in_specs.append(pl.BlockSpec(memory_space=pl.ANY))
        args.append(mix)
        aliases = {len(args) - 1: 0}
    return pl.pallas_call(
        kern,
        out_shape=(y_shape, jax.ShapeDtypeStruct((NB, H, dk, dvb), F32)),
        grid=(NB, H // hp),
        in_specs=in_specs,
        out_specs=(pl.BlockSpec((T, wv), lambda b, g: (b, mix_col0 // wv + g)),
                   pl.BlockSpec((None, hp, dk, dvb), lambda b, g: (b, g, 0, 0))),
        input_output_aliases=aliases,
        compiler_params=_params(("parallel", "parallel")),
        name="gla",
    )(*args)


def _lru_kernel(cx_ref, cg_ref, buf_ref, h0_ref, cw_ref, cb_ref, wa_ref, ba_ref, wx_ref, bx_ref, lam_ref,
                *rest, T, cw, bw, conv_w):
    y_ref, hl_ref, nb_ref, e_ref, a_ref, x_ref = rest[-6:]
    seg = T // SUBLANES
    e_ref[0:SUBLANES, :] = buf_ref[...]
    e_ref[SUBLANES:SUBLANES + T, :] = cx_ref[...]
    first = SUBLANES - (conv_w - 1)
    u = cb_ref[...] + cw_ref[0:1, :] * e_ref[first:first + T, :]
    for j in range(1, conv_w):
        u = u + cw_ref[j:j + 1, :] * e_ref[first + j:first + j + T, :]
    nb_ref[...] = e_ref[T:T + SUBLANES, :]

    lam = lam_ref[...]
    sp = jnp.maximum(-lam, 0.0) + jnp.log1p(jnp.exp(-jnp.abs(lam)))
    for n in range(cw // bw):
        sl = slice(n * bw, (n + 1) * bw)
        un = u[:, sl]
        ub = un.astype(BF16)
        r = jax.nn.sigmoid(jnp.dot(ub, wa_ref[n].astype(BF16), preferred_element_type=F32) + ba_ref[:, sl])
        i = jax.nn.sigmoid(jnp.dot(ub, wx_ref[n].astype(BF16), preferred_element_type=F32) + bx_ref[:, sl])
        log_a = (-LRU_C) * r * sp[:, sl]
        a = jnp.exp(log_a)
        a_ref[n] = a
        x_ref[n] = jnp.sqrt((1.0 - a) * (1.0 + a)) * (i * un)

    nl = cw // bw
    rows = lambda t: pl.ds(t, SUBLANES, stride=seg)
    sub = lax.broadcasted_iota(jnp.int32, (SUBLANES, bw), 0)
    bcast = lambda v, s: jnp.broadcast_to(v[s:s + 1, :], (SUBLANES, bw))

    def pass1(t, carry):
        hs, ps = carry
        new_h, new_p = [], []
        for n in range(nl):
            a8 = a_ref[n, rows(t), :]
            h = a8 * hs[n] + x_ref[n, rows(t), :]
            p = a8 * ps[n]
            x_ref[n, rows(t), :] = h
            a_ref[n, rows(t), :] = p
            new_h.append(h)
            new_p.append(p)
        return tuple(new_h), tuple(new_p)

    h_init = tuple(jnp.where(sub == 0, jnp.broadcast_to(h0_ref[:, n * bw:(n + 1) * bw], (SUBLANES, bw)), 0.0)
                   for n in range(nl))
    p_init = tuple(jnp.ones((SUBLANES, bw), F32) for _ in range(nl))
    h_end, p_end = lax.fori_loop(0, seg, pass1, (h_init, p_init))

    carry_in = []
    for n in range(nl):
        e = bcast(h_end[n], 0)
        c = jnp.zeros((SUBLANES, bw), F32)
        for s in range(1, SUBLANES):
            c = jnp.where(sub == s, e, c)
            e = bcast(h_end[n], s) + bcast(p_end[n], s) * e
        carry_in.append(c)
        hl_ref[:, n * bw:(n + 1) * bw] = e[0:1, :]

    def pass2(t, _):
        for n in range(nl):
            x_ref[n, rows(t), :] = x_ref[n, rows(t), :] + a_ref[n, rows(t), :] * carry_in[n]
        return 0

    lax.fori_loop(0, seg, pass2, 0)
    for n in range(nl):
        sl = slice(n * bw, (n + 1) * bw)
        y_ref[:, sl] = (x_ref[n] * jax.nn.gelu(cg_ref[:, sl], approximate=True)).astype(y_ref.dtype)


def rglru(zc, bufp, h0, conv_w, conv_b, w_a, b_a, w_x, b_x, lam, layer, *, row0, NB, T, d_lru,
          mix=None, mix_col0=0):
    bw = w_a.shape[2]
    assert bw == LANES and T % SUBLANES == 0
    cw = _tile(d_lru, 512, bw)
    ncg = d_lru // cw
    nbk = cw // bw
    cwid = conv_w.shape[1]
    rb = row0 // T
    assert mix_col0 % cw == 0
    kern = functools.partial(_lru_kernel, T=T, cw=cw, bw=bw, conv_w=cwid)
    vec = lambda rows: pl.BlockSpec((None, rows, cw), lambda b, j: (layer, 0, j))
    in_specs = [pl.BlockSpec((T, cw), lambda b, j: (rb + b, j)),
                pl.BlockSpec((T, cw), lambda b, j: (rb + b, ncg + j)),
                pl.BlockSpec((None, SUBLANES, cw), lambda b, j: (b, 0, j)),
                pl.BlockSpec((None, 1, cw), lambda b, j: (b, 0, j)),
                vec(cwid), vec(1),
                pl.BlockSpec((None, nbk, bw, bw), lambda b, j: (layer, j, 0, 0)), vec(1),
                pl.BlockSpec((None, nbk, bw, bw), lambda b, j: (layer, j, 0, 0)), vec(1), vec(1)]
    args = [zc, zc, bufp, h0, conv_w, conv_b, w_a, b_a, w_x, b_x, lam]
    aliases = {}
    if mix is None:
        y_shape = jax.ShapeDtypeStruct((NB * T, d_lru), F32)
    else:
        y_shape = jax.ShapeDtypeStruct(mix.shape, mix.dtype)
        in_specs.append(pl.BlockSpec(memory_space=pl.ANY))
        args.append(mix)
        aliases = {len(args) - 1: 0}
    return pl.pallas_call(
        kern,
        out_shape=(y_shape,
                   jax.ShapeDtypeStruct((NB, 1, d_lru), F32),
                   jax.ShapeDtypeStruct((NB, SUBLANES, d_lru), F32)),
        grid=(NB, ncg),
        in_specs=in_specs,
        out_specs=(pl.BlockSpec((T, cw), lambda b, j: (b, mix_col0 // cw + j)),
                   pl.BlockSpec((None, 1, cw), lambda b, j: (b, 0, j)),
                   pl.BlockSpec((None, SUBLANES, cw), lambda b, j: (b, 0, j))),
        scratch_shapes=[pltpu.VMEM((T + SUBLANES, cw), F32), pltpu.VMEM((nbk, T, bw), F32),
                        pltpu.VMEM((nbk, T, bw), F32)],
        input_output_aliases=aliases,
        compiler_params=_params(("parallel", "parallel")),
        name="rglru",
    )(*args)


def kernel(x_prompt, x_sample, cache_k, cache_v, state_gla, state_lru_h, state_lru_conv, page_table, meta_tokens, norm_mix_pre, norm_mix_post, norm_mlp_pre, norm_mlp_post, w_in, w_out, lam_q1, lam_k1, lam_q2, lam_k2, attn_subln, gla_w_gate, gla_b_gate, gla_norm, lru_conv_w, lru_conv_b, lru_w_a, lru_b_a, lru_w_x, lru_b_x, lru_lambda, mlp_w_up, mlp_w_down):
    B, seq, D = x_prompt.shape
    DB, S, _ = x_sample.shape
    nm = meta_tokens.shape[0]
    T = nm + seq
    depth = w_in.shape[0]
    H_A, dv_a = cache_k.shape[3], cache_v.shape[4]
    dh_a = dv_a // 2
    d_a = H_A * dv_a
    H_B, dk_b, dv_b = state_gla.shape[2:]
    d_b = H_B * dv_b
    rank = gla_w_gate.shape[1]
    d_lru = state_lru_h.shape[-1]
    conv_w = lru_conv_w.shape[1]
    d_ff = mlp_w_up.shape[2]
    R_P, R_S = B * T, DB * S
    assert nm <= LANES and nm % 16 == 0 and seq % LANES == 0 and S <= SUBLANES and R_P % S == 0
    assert dv_a == LANES and dk_b == LANES and dv_b % LANES == 0 and rank <= LANES

    c_ak, c_av = d_a, 2 * d_a
    c_bq = 3 * d_a
    c_bk = c_bq + H_B * dk_b
    c_bv = c_bk + H_B * dk_b
    c_bg = c_bv + d_b
    c_br = c_bg + d_b
    c_cx = c_br + rank
    n_ab = c_br
    gla_cols = (c_bq, c_bk, c_bv, c_bg)
    d_mix = d_a + d_b + d_lru

    w_in_t = jnp.swapaxes(w_in, 1, 2)
    wg = jnp.pad(gla_w_gate, ((0, 0), (0, LANES - rank), (0, 0)))
    row3 = lambda a: a.reshape(depth, 1, a.shape[-1])

    slopes = jnp.exp2(-8.0 * jnp.arange(1, H_A + 1, dtype=F32) / H_A)
    slope_col = jnp.repeat(slopes, 2 * S).reshape(H_A * 2 * S, 1)
    eye_c = jnp.eye(2, dtype=F32)
    assert S * H_A <= LANES

    blocks_p = ((0, nm, LANES - nm),) + tuple((nm + LANES * n, LANES, 0) for n in range(seq // LANES))
    blocks_s = ((0, S, 0),)

    h = jnp.concatenate([
        jnp.concatenate([jnp.broadcast_to(meta_tokens[None], (B, nm, D)), x_prompt], axis=1).reshape(R_P, D),
        x_sample.reshape(R_S, D)], axis=0)
    zeros_gla = jnp.zeros((B, H_B, dk_b, dv_b), F32)
    zeros_h = jnp.zeros((B, 1, d_lru), F32)
    zeros_buf = jnp.zeros((B, SUBLANES, d_lru), F32)

    outs = [[] for _ in range(8)]
    kv_prompt = None
    hn = rms_cast(h, norm_mix_pre[0:1])
    for l in range(depth):
        lam_init = 0.8 - 0.6 * math.exp(-0.3 * l)
        lamp = jnp.stack([lam_q1[l], lam_k1[l], lam_q2[l], lam_k2[l]])
        subln = attn_subln[l:l + 1]

        tm = 2080 if l == 0 else 1040
        kt = dict(tn_target=1024, tk_target=1024) if l == 0 else dict(tn_target=2048, tk_target=512)
        z = matmul_nt(hn, w_in_t, l, 0, n_ab, tm_target=tm)
        zc = matmul_nt(hn, w_in_t, l, c_cx, 2 * d_lru, tm_target=tm)
        zr = matmul_nt(hn, w_in_t, l, c_br, rank)
        kv_prompt = kv_export(z, kv_prompt, l, depth=depth, B=B, T=T, H=H_A, dv=dv_a)

        mix = attn_prompt(z, slopes, lamp, subln, B=B, T=T, nm=nm, H=H_A, dh=dh_a, lam_init=lam_init,
                          mix_shape=(R_P + R_S, d_mix))
        q_s = z[R_P:, :d_a].reshape(DB, S, H_A, 2, dh_a) * (dh_a ** -0.5)
        qrows = jnp.einsum('bihcd,cy->bhciyd', q_s, eye_c).reshape(DB, H_A * 2 * S, dv_a).astype(BF16)
        new_rows = lambda c0: jnp.pad(z[R_P:, c0:c0 + d_a].reshape(DB, S * H_A, dv_a),
                                      ((0, 0), (0, LANES - S * H_A), (0, 0)))
        oa_s = attn_sample(qrows, new_rows(c_ak), new_rows(c_av), slope_col, lamp, subln, cache_k, cache_v,
                           page_table, l, DB=DB, S=S, H=H_A, dh=dh_a, lam_init=lam_init)
        gla_args = (wg, row3(gla_b_gate), row3(gla_norm))
        mix, sg_p = gla(z, zr, *gla_args, zeros_gla, l, row0=0, NB=B, T=T, H=H_B, dk=dk_b, dvb=dv_b,
                        cols=gla_cols, blocks=blocks_p, mix=mix, mix_col0=d_a)
        ob_s, sg_s = gla(z, zr, *gla_args, state_gla[l], l, row0=R_P, NB=DB, T=S, H=H_B, dk=dk_b, dvb=dv_b,
                         cols=gla_cols, blocks=blocks_s)
        lru_args = (lru_conv_w, row3(lru_conv_b), lru_w_a, row3(lru_b_a), lru_w_x, row3(lru_b_x),
                    row3(lru_lambda))
        mix, hl_p, nb_p = rglru(zc, zeros_buf, zeros_h, *lru_args, l, row0=0, NB=B, T=T, d_lru=d_lru,
                                mix=mix, mix_col0=d_a + d_b)
        bufp = jnp.pad(state_lru_conv[l], ((0, 0), (SUBLANES - (conv_w - 1), 0), (0, 0)))
        oc_s, hl_s, nb_s = rglru(zc, bufp, state_lru_h[l][:, None, :], *lru_args, l, row0=R_P, NB=DB, T=S,
                                 d_lru=d_lru)

        mix = lax.dynamic_update_slice(mix, jnp.concatenate([oa_s, ob_s, oc_s], axis=1).astype(BF16), (R_P, 0))
        m1 = matmul(mix, w_out, l, D, F32, tm_target=tm)
        h, hn2 = resid_norm(h, m1, norm_mix_post[l:l + 1], norm_mlp_pre[l:l + 1])
        up = matmul(hn2, mlp_w_up, l, d_ff, BF16, relu2=True, tm_target=tm)
        dn = matmul_ktiled(up, mlp_w_down, l, **kt)
        if l + 1 < depth:
            h, hn = resid_norm(h, dn, norm_mlp_post[l:l + 1], norm_mix_pre[l + 1:l + 2])

        new = (z[R_P:, c_ak:c_av].reshape(DB, S, H_A, dv_a), z[R_P:, c_av:c_bq].reshape(DB, S, H_A, dv_a),
               sg_p, sg_s, hl_p[:, 0, :], hl_s[:, 0, :],
               nb_p[:, SUBLANES - (conv_w - 1):, :], nb_s[:, SUBLANES - (conv_w - 1):, :])
        for acc, val in zip(outs, new):
            acc.append(val)

    gpost = norm_mlp_post[depth - 1:depth]
    y_prompt = resid_prompt_out(h, dn, gpost, B=B, T=T, nm=nm)
    y_sample = resid_norm(h[R_P:], dn[R_P:], gpost)[0].reshape(DB, S, D)
    k_prompt, v_prompt = (a.reshape(depth, B, T, H_A, dv_a) for a in kv_prompt)
    return (y_prompt, y_sample, k_prompt, v_prompt) + tuple(jnp.stack(o) for o in outs)
```

```python
import functools
import math

import jax
import jax.numpy as jnp
from jax import lax
from jax.experimental import pallas as pl
from jax.experimental.pallas import tpu as pltpu

F32 = jnp.float32
BF16 = jnp.bfloat16
EPS = 1e-6
GLA_TAU = 16.0
LRU_C = 8.0
NEG = -1e30
LANES = 128
SUBLANES = 8
VMEM_LIMIT_BYTES = 56 * 1024 * 1024

_NT = (((1,), (1,)), ((), ()))


def _tile(n, target, mult):
    best = None
    for d in range(mult, min(n, target) + 1, mult):
        if n % d == 0:
            best = d
    return n if best is None else best


def _params(sem):
    return pltpu.CompilerParams(dimension_semantics=sem, vmem_limit_bytes=VMEM_LIMIT_BYTES)


def _rms(x, g):
    return x * lax.rsqrt(jnp.mean(x * x, axis=-1, keepdims=True) + EPS) * g


def _rms_cast_kernel(x_ref, g_ref, o_ref):
    o_ref[...] = _rms(x_ref[...], g_ref[...]).astype(o_ref.dtype)


def rms_cast(x, g):
    R, D = x.shape
    tr = _tile(R, 416, 16)
    return pl.pallas_call(
        _rms_cast_kernel,
        out_shape=jax.ShapeDtypeStruct((R, D), BF16),
        grid=(R // tr,),
        in_specs=[pl.BlockSpec((tr, D), lambda i: (i, 0)), pl.BlockSpec((1, D), lambda i: (0, 0))],
        out_specs=pl.BlockSpec((tr, D), lambda i: (i, 0)),
        compiler_params=_params(("parallel",)),
        name="rms_cast",
    )(x, g)


def _resid_kernel(h_ref, m_ref, gpost_ref, o_ref):
    o_ref[...] = h_ref[...] + _rms(m_ref[...], gpost_ref[...])


def _resid_next_kernel(h_ref, m_ref, gpost_ref, gpre_ref, o_ref, hn_ref):
    h = h_ref[...] + _rms(m_ref[...], gpost_ref[...])
    o_ref[...] = h
    hn_ref[...] = _rms(h, gpre_ref[...]).astype(hn_ref.dtype)


def resid_norm(h, m, gpost, gpre=None):
    R, D = h.shape
    tr = _tile(R, 208, 16)
    row = pl.BlockSpec((tr, D), lambda i: (i, 0))
    vec = pl.BlockSpec((1, D), lambda i: (0, 0))
    if gpre is None:
        return pl.pallas_call(
            _resid_kernel, out_shape=jax.ShapeDtypeStruct((R, D), F32), grid=(R // tr,),
            in_specs=[row, row, vec], out_specs=row, compiler_params=_params(("parallel",)),
            name="resid_norm")(h, m, gpost), None
    return pl.pallas_call(
        _resid_next_kernel,
        out_shape=(jax.ShapeDtypeStruct((R, D), F32), jax.ShapeDtypeStruct((R, D), BF16)),
        grid=(R // tr,), in_specs=[row, row, vec, vec], out_specs=(row, row),
        compiler_params=_params(("parallel",)), name="resid_norm_next")(h, m, gpost, gpre)


def resid_prompt_out(h, m, gpost, *, B, T, nm):
    D = h.shape[1]
    seq = T - nm
    tr = _tile(seq, 256, SUBLANES)
    rows = pl.BlockSpec((pl.Element(tr), pl.Element(D)),
                        lambda b, i: (pl.multiple_of(b * T + nm + i * tr, SUBLANES), 0))
    return pl.pallas_call(
        _resid_kernel, out_shape=jax.ShapeDtypeStruct((B, seq, D), F32), grid=(B, seq // tr),
        in_specs=[rows, rows, pl.BlockSpec((1, D), lambda b, i: (0, 0))],
        out_specs=pl.BlockSpec((None, tr, D), lambda b, i: (b, i, 0)),
        compiler_params=_params(("parallel", "parallel")), name="resid_prompt_out")(h, m, gpost)


def _kv_export_kernel(k_ref, v_ref, *rest, H, dv):
    ko_ref, vo_ref = rest[-2:]
    tt = k_ref.shape[0]
    for h in range(H):
        ko_ref[pl.ds(h, tt, stride=H), :] = k_ref[:, h * dv:(h + 1) * dv]
        vo_ref[pl.ds(h, tt, stride=H), :] = v_ref[:, h * dv:(h + 1) * dv]


def kv_export(z, prev, layer, *, depth, B, T, H, dv):
    tt = _tile(T, 344, SUBLANES)
    nt = T // tt
    out = jax.ShapeDtypeStruct((depth * B * T * H, dv), F32)
    in_specs = [pl.BlockSpec((tt, H * dv), lambda b, i: (b * nt + i, 1)),
                pl.BlockSpec((tt, H * dv), lambda b, i: (b * nt + i, 2))]
    args, aliases = [z, z], {}
    if prev is not None:
        in_specs += [pl.BlockSpec(memory_space=pl.ANY)] * 2
        args += list(prev)
        aliases = {2: 0, 3: 1}
    out_spec = pl.BlockSpec((tt * H, dv), lambda b, i: ((layer * B + b) * nt + i, 0))
    return pl.pallas_call(
        functools.partial(_kv_export_kernel, H=H, dv=dv), out_shape=(out, out), grid=(B, nt),
        in_specs=in_specs, out_specs=(out_spec, out_spec), input_output_aliases=aliases,
        compiler_params=_params(("parallel", "parallel")), name="kv_export")(*args)


def _mm_kernel(x_ref, w_ref, o_ref, *, relu2):
    acc = jnp.dot(x_ref[...], w_ref[...].astype(BF16), preferred_element_type=F32)
    if relu2:
        acc = jnp.square(jnp.maximum(acc, 0.0))
    o_ref[...] = acc.astype(o_ref.dtype)


def _x_block(tm, K, tm_target):
    mode = dict(pipeline_mode=pl.Buffered(1)) if tm_target > 1040 else {}
    return pl.BlockSpec((tm, K), lambda i, j: (i, 0), **mode)


def matmul(x, w, layer, n_out, out_dtype, relu2=False, tm_target=1040, tn_target=512):
    R, K = x.shape
    tm = _tile(R, tm_target, 16)
    tn = _tile(n_out, tn_target, LANES)
    return pl.pallas_call(
        functools.partial(_mm_kernel, relu2=relu2),
        out_shape=jax.ShapeDtypeStruct((R, n_out), out_dtype),
        grid=(R // tm, n_out // tn),
        in_specs=[_x_block(tm, K, tm_target),
                  pl.BlockSpec((None, K, tn), lambda i, j: (layer, 0, j))],
        out_specs=pl.BlockSpec((tm, tn), lambda i, j: (i, j)),
        compiler_params=_params(("parallel", "parallel")),
        name="matmul",
    )(x, w)


def _mm_nt_kernel(x_ref, w_ref, o_ref):
    w = w_ref[0].astype(BF16)
    pad = o_ref.shape[1] - w.shape[0]
    if pad:
        w = jnp.concatenate([w, jnp.zeros((pad, w.shape[1]), BF16)], axis=0)
    o_ref[...] = lax.dot_general(x_ref[...], w, _NT, preferred_element_type=F32)


def matmul_nt(x, wt, layer, row0, n_rows, tm_target=1040, tn_target=512):
    R, K = x.shape
    assert row0 % SUBLANES == 0
    tm = _tile(R, tm_target, 16)
    tn = _tile(n_rows, tn_target, LANES) if n_rows >= LANES else n_rows
    n_out = max(n_rows, LANES)
    to = max(tn, LANES)
    return pl.pallas_call(
        _mm_nt_kernel,
        out_shape=jax.ShapeDtypeStruct((R, n_out), F32),
        grid=(R // tm, n_rows // tn),
        in_specs=[_x_block(tm, K, tm_target),
                  pl.BlockSpec((pl.Element(1), pl.Element(tn), pl.Element(K)),
                               lambda i, j: (layer, pl.multiple_of(row0 + j * tn, SUBLANES), 0))],
        out_specs=pl.BlockSpec((tm, to), lambda i, j: (i, j)),
        compiler_params=_params(("parallel", "parallel")),
        name="matmul_nt",
    )(x, wt)


def _mmk_kernel(x_ref, w_ref, o_ref):
    @pl.when(pl.program_id(2) == 0)
    def _():
        o_ref[...] = jnp.zeros_like(o_ref)

    o_ref[...] += jnp.dot(x_ref[...], w_ref[...].astype(BF16), preferred_element_type=F32)


def matmul_ktiled(x, w, layer, tm_target=2080, tn_target=1024, tk_target=1024):
    R, K = x.shape
    N = w.shape[2]
    tm = _tile(R, tm_target, 16)
    tn = _tile(N, tn_target, LANES)
    tk = _tile(K, tk_target, LANES)
    return pl.pallas_call(
        _mmk_kernel,
        out_shape=jax.ShapeDtypeStruct((R, N), F32),
        grid=(R // tm, N // tn, K // tk),
        in_specs=[pl.BlockSpec((tm, tk), lambda i, j, k: (i, k)),
                  pl.BlockSpec((None, tk, tn), lambda i, j, k: (layer, k, j))],
        out_specs=pl.BlockSpec((tm, tn), lambda i, j, k: (i, j)),
        compiler_params=_params(("parallel", "parallel", "arbitrary")),
        name="matmul_ktiled",
    )(x, w)


def _lam_value(lamp_ref, lam_init):
    lp = lamp_ref[...]
    a = jnp.exp(jnp.sum(lp[0:1] * lp[1:2], axis=-1, keepdims=True))
    b = jnp.exp(jnp.sum(lp[2:3] * lp[3:4], axis=-1, keepdims=True))
    return a - b + lam_init


def _attn_prompt_kernel(slopes_ref, lamp_ref, subln_ref, subln_col_ref, q_ref, k_ref, v_ref, o_ref,
                        kb_ref, vt_ref, *s_refs, nm, qb, nblk, dh, lam_init):
    T = q_ref.shape[0]
    dv = 2 * dh
    pad = LANES - nm
    log2e = math.log2(math.e)
    slope = slopes_ref[pl.program_id(1)] * log2e
    lam = _lam_value(lamp_ref, lam_init)
    scale = dh ** -0.5 * log2e

    lane = lax.broadcasted_iota(jnp.int32, (1, dv), 1)
    first_map = lane < dh

    kb_ref[0:pad, 0:dv] = jnp.zeros((pad, dv), BF16)
    kb_ref[pad:pad + T, 0:dv] = k_ref[...].astype(BF16)
    kbias = slope * (lax.broadcasted_iota(jnp.int32, (pad + T, dv), 0) - pad).astype(F32)
    hi = kbias.astype(BF16).astype(F32)
    mid = (kbias - hi).astype(BF16).astype(F32)
    lo = kbias - hi - mid
    kb_ref[:, dv:2 * dv] = jnp.where(lane == 0, hi, jnp.where(lane == 1, mid, jnp.where(lane == 2, lo, 0.0))
                                     ).astype(BF16)
    vt_ref[...] = jnp.concatenate([jnp.zeros((pad, dv), F32), v_ref[...]], axis=0).T.astype(BF16)
    ones3 = jnp.where(lane < 3, 1.0, 0.0)

    def stacked_q(q):
        qs = q * scale
        qq = jnp.concatenate([jnp.where(first_map, qs, 0.0), jnp.where(first_map, 0.0, qs)], axis=0)
        return jnp.concatenate([qq, jnp.broadcast_to(ones3, qq.shape)], axis=1).astype(BF16)

    qq = stacked_q(q_ref[0:nm, :])
    s = lax.dot_general(qq, kb_ref[0:LANES, :], _NT, preferred_element_type=F32)
    col = lax.broadcasted_iota(jnp.int32, (2 * nm, LANES), 1)
    row = lax.broadcasted_iota(jnp.int32, (2 * nm, LANES), 0)
    row = jnp.where(row >= nm, row - nm, row)
    kpos = col - pad
    s = jnp.where((kpos >= 0) & (kpos <= row), s, NEG)
    m = jnp.max(s, axis=-1, keepdims=True)
    p = jnp.exp2(s - m)
    l = jnp.sum(p, axis=-1, keepdims=True)
    o = lax.dot_general(p.astype(BF16), vt_ref[:, 0:LANES], _NT, preferred_element_type=F32) / l
    res = o[:nm] - lam * o[nm:]
    o_ref[0:nm, :] = (_rms(res, subln_ref[...]) * (1.0 - lam_init)).astype(o_ref.dtype)

    key_ok = lax.broadcasted_iota(jnp.int32, (LANES, 2 * qb), 0) >= pad
    ql = lax.broadcasted_iota(jnp.int32, (qb, 2 * qb), 1)
    ql = jnp.where(ql >= qb, ql - qb, ql)
    causal = lax.broadcasted_iota(jnp.int32, (qb, 2 * qb), 0) <= ql

    for qj in range(nblk):
        r0 = nm + qj * qb
        nk = LANES + (qj + 1) * qb
        qq = stacked_q(q_ref[r0:r0 + qb, :])
        s = lax.dot_general(kb_ref[0:nk, :], qq, _NT, preferred_element_type=F32)
        parts = [jnp.where(key_ok, s[0:LANES, :], NEG)]
        if nk - qb > LANES:
            parts.append(s[LANES:nk - qb, :])
        parts.append(jnp.where(causal, s[nk - qb:nk, :], NEG))
        s_ref = s_refs[qj % 2]
        s_ref[0:nk, :] = jnp.concatenate(parts, axis=0)
        m = jnp.max(s_ref[0:nk, :], axis=0, keepdims=True)
        p = jnp.exp2(s_ref[0:nk, :] - m)
        l = jnp.sum(p, axis=0, keepdims=True)
        o = jnp.dot(vt_ref[:, 0:nk], p.astype(BF16), preferred_element_type=F32) / l
        res = o[:, :qb] - lam * o[:, qb:]
        y = res * lax.rsqrt(jnp.mean(res * res, axis=0, keepdims=True) + EPS) * subln_col_ref[...]
        o_ref[r0:r0 + qb, :] = (y * (1.0 - lam_init)).T.astype(o_ref.dtype)


def attn_prompt(z, slopes, lamp, subln, *, B, T, nm, H, dh, lam_init, mix_shape):
    seq = T - nm
    qb = _tile(seq, 256, LANES)
    nblk = seq // qb
    dv = 2 * dh
    kern = functools.partial(_attn_prompt_kernel, nm=nm, qb=qb, nblk=nblk, dh=dh, lam_init=lam_init)
    return pl.pallas_call(
        kern,
        out_shape=jax.ShapeDtypeStruct(mix_shape, BF16),
        grid=(B, H),
        in_specs=[pl.BlockSpec(memory_space=pltpu.SMEM),
                  pl.BlockSpec((4, dh), lambda b, h: (0, 0)),
                  pl.BlockSpec((1, dv), lambda b, h: (0, 0)),
                  pl.BlockSpec((dv, 1), lambda b, h: (0, 0)),
                  pl.BlockSpec((T, dv), lambda b, h: (b, h)),
                  pl.BlockSpec((T, dv), lambda b, h: (b, H + h)),
                  pl.BlockSpec((T, dv), lambda b, h: (b, 2 * H + h))],
        out_specs=pl.BlockSpec((T, dv), lambda b, h: (b, h)),
        scratch_shapes=[pltpu.VMEM((LANES + seq, 2 * dv), BF16), pltpu.VMEM((dv, LANES + seq), BF16),
                        pltpu.VMEM((LANES + seq, 2 * qb), F32), pltpu.VMEM((LANES + seq, 2 * qb), F32)],
        compiler_params=_params(("parallel", "parallel")),
        name="attn_prompt",
    )(slopes, lamp, subln, subln.reshape(dv, 1), z, z, z)


def _attn_sample_kernel(pt_ref, lamp_ref, subln_ref, slope_ref, q_ref, kn_ref, vn_ref, *rest,
                        G, page, past_len, H, S, dv, lam_init):
    k_refs, v_refs = rest[:G], rest[G:2 * G]
    o_ref, m_ref, l_ref, acc_ref, bias_ref = rest[2 * G:]
    j = pl.program_id(1)
    nrow = H * 2 * S
    h_shift, s_shift = H.bit_length() - 1, (2 * S).bit_length() - 1
    row = lax.broadcasted_iota(jnp.int32, (nrow, 1), 0)
    row_head = lax.shift_right_logical(row, s_shift)
    slope = slope_ref[...]

    def block_bias(n, limit=None):
        col = lax.broadcasted_iota(jnp.int32, (1, n), 1)
        key = lax.shift_right_logical(col, h_shift)
        ok = (col & (H - 1)) == row_head
        if limit is not None:
            ok = ok & (key <= limit)
        return jnp.where(ok, slope * key.astype(F32), NEG)

    @pl.when(j == 0)
    def _():
        m_ref[...] = jnp.full(m_ref.shape, NEG, F32)
        l_ref[...] = jnp.zeros(l_ref.shape, F32)
        acc_ref[...] = jnp.zeros(acc_ref.shape, F32)
        bias_ref[...] = block_bias(bias_ref.shape[1])

    def update(kk, vv, bias, shift):
        s = lax.dot_general(q_ref[...], kk, _NT, preferred_element_type=F32) + bias
        m_old = m_ref[...]
        m_new = jnp.maximum(m_old, jnp.max(s, axis=-1, keepdims=True) + shift)
        alpha = jnp.exp(m_old - m_new)
        p = jnp.exp(s - (m_new - shift))
        l_ref[...] = alpha * l_ref[...] + jnp.sum(p, axis=-1, keepdims=True)
        acc_ref[...] = alpha * acc_ref[...] + jnp.dot(p.astype(BF16), vv, preferred_element_type=F32)
        m_ref[...] = m_new

    kk = jnp.concatenate([r[...].astype(BF16) for r in k_refs], axis=0)
    vv = jnp.concatenate([r[...].astype(BF16) for r in v_refs], axis=0)
    update(kk, vv, bias_ref[...], slope * (j * (G * page) - past_len).astype(F32))

    @pl.when(j == pl.num_programs(1) - 1)
    def _():
        update(kn_ref[...].astype(BF16), vn_ref[...].astype(BF16),
               block_bias(kn_ref.shape[0], row & (S - 1)), 0.0)
        lam = _lam_value(lamp_ref, lam_init)
        o = acc_ref[...] / l_ref[...]
        for h in range(H):
            blk = o[h * 2 * S:(h + 1) * 2 * S, :]
            res = blk[:S] - lam * blk[S:]
            o_ref[:, h * dv:(h + 1) * dv] = _rms(res, subln_ref[...]) * (1.0 - lam_init)


def attn_sample(qrows, knew, vnew, slope_col, lamp, subln, cache_k, cache_v, page_table, layer, *,
                DB, S, H, dh, lam_init):
    dv = 2 * dh
    n_pages = page_table.shape[1]
    page = cache_k.shape[2]
    assert H & (H - 1) == 0 and S & (S - 1) == 0
    G = _tile(n_pages, 8, 1)
    ck = cache_k.reshape(cache_k.shape[0], cache_k.shape[1], page * H, dv)
    cv = cache_v.reshape(cache_v.shape[0], cache_v.shape[1], page * H, dv)
    nrow = H * 2 * S

    def page_spec(g):
        return pl.BlockSpec((None, None, page * H, dv), lambda b, j, pt: (layer, pt[b, j * G + g], 0, 0))

    kern = functools.partial(_attn_sample_kernel, G=G, page=page, past_len=n_pages * page,
                             H=H, S=S, dv=dv, lam_init=lam_init)
    per_b = lambda rows: pl.BlockSpec((None, rows, dv), lambda b, j, pt: (b, 0, 0))
    grid_spec = pltpu.PrefetchScalarGridSpec(
        num_scalar_prefetch=1,
        grid=(DB, n_pages // G),
        in_specs=[pl.BlockSpec((4, dh), lambda b, j, pt: (0, 0)),
                  pl.BlockSpec((1, dv), lambda b, j, pt: (0, 0)),
                  pl.BlockSpec((nrow, 1), lambda b, j, pt: (0, 0)),
                  per_b(nrow), per_b(knew.shape[1]), per_b(vnew.shape[1])]
                 + [page_spec(g) for g in range(G)] + [page_spec(g) for g in range(G)],
        out_specs=pl.BlockSpec((S, H * dv), lambda b, j, pt: (b, 0)),
        scratch_shapes=[pltpu.VMEM((nrow, 1), F32), pltpu.VMEM((nrow, 1), F32),
                        pltpu.VMEM((nrow, dv), F32), pltpu.VMEM((nrow, G * page * H), F32)],
    )
    return pl.pallas_call(
        kern, out_shape=jax.ShapeDtypeStruct((DB * S, H * dv), F32), grid_spec=grid_spec,
        compiler_params=_params(("parallel", "arbitrary")), name="attn_sample",
    )(page_table, lamp, subln, slope_col, qrows, knew, vnew, *([ck] * G), *([cv] * G))


def _gla_kernel(q_ref, k_ref, v_ref, g_ref, r_ref, wg_ref, bg_ref, norm_ref, s0_ref, *rest,
                blocks, dk, dvb, hp):
    o_ref, sout_ref = rest[-2:]
    L = LANES
    ri = lax.broadcasted_iota(jnp.int32, (L, L), 0)
    ci = lax.broadcasted_iota(jnp.int32, (L, L), 1)
    tril = ri >= ci
    tril_bf = jnp.where(tril, 1.0, 0.0).astype(BF16)
    rowi = lax.broadcasted_iota(jnp.int32, (L, 1), 0)
    qscale = dk ** -0.5
    states = [s0_ref[hh] for hh in range(hp)]

    def load(ref, row0, nreal, off, c0, width):
        x = ref[row0:row0 + nreal, c0:c0 + width]
        if nreal == L:
            return x
        parts = []
        if off:
            parts.append(jnp.zeros((off, width), x.dtype))
        parts.append(x)
        if L - off - nreal:
            parts.append(jnp.zeros((L - off - nreal, width), x.dtype))
        return jnp.concatenate(parts, axis=0)

    r_all = jnp.concatenate([load(r_ref, *blk, 0, L) for blk in blocks], axis=0).astype(BF16)
    x_all = jnp.dot(r_all, wg_ref[...].astype(BF16), preferred_element_type=F32) + bg_ref[...]
    gk_all = (jnp.minimum(x_all, 0.0) - jnp.log1p(jnp.exp(-jnp.abs(x_all)))) * (1.0 / GLA_TAU)
    pieces = []
    for n, (row0, nreal, off) in enumerate(blocks):
        g_n = gk_all[n * L:(n + 1) * L, :]
        if nreal < L:
            g_n = jnp.where((rowi >= off) & (rowi < off + nreal), g_n, 0.0)
        pieces.append(g_n)
    g_wide = jnp.concatenate(pieces, axis=1)
    hi = g_wide.astype(BF16)
    lo = (g_wide - hi.astype(F32)).astype(BF16)
    b_wide = (jnp.dot(tril_bf, hi, preferred_element_type=F32)
              + jnp.dot(tril_bf, lo, preferred_element_type=F32))

    for n, (row0, nreal, off) in enumerate(blocks):
        for hh in range(hp):
            S = states[hh]
            q = load(q_ref, row0, nreal, off, hh * dk, dk) * qscale
            k = load(k_ref, row0, nreal, off, hh * dk, dk)
            v = load(v_ref, row0, nreal, off, hh * dvb, dvb).astype(BF16)
            b = b_wide[:, (n * hp + hh) * dk:(n * hp + hh + 1) * dk]
            b_last = b[L - 1:L, :]
            b_mid = b[L // 2 - 1:L // 2, :]
            qs = (q * jnp.exp(b)).astype(BF16)
            qm = (q * jnp.exp(b - b_mid)).astype(BF16)
            km = (k * jnp.exp(b_mid - b)).astype(BF16)
            att = lax.dot_general(qm, km, _NT, preferred_element_type=F32)
            att = jnp.where(tril, att, 0.0).astype(BF16)
            o = jnp.dot(jnp.concatenate([att, qs], axis=1),
                        jnp.concatenate([v, S.astype(BF16)], axis=0), preferred_element_type=F32)
            kk = (k * jnp.exp(b_last - b)).T.astype(BF16)
            decay = jnp.broadcast_to(jnp.exp(b_last), (L, dk)).T
            states[hh] = (jnp.concatenate([decay] * (dvb // L), axis=1) * S
                          + jnp.dot(kk, v, preferred_element_type=F32))
            y = _rms(o, norm_ref[...])
            g = load(g_ref, row0, nreal, off, hh * dvb, dvb)
            y = y * (g / (1.0 + jnp.exp(-g)))
            o_ref[row0:row0 + nreal, hh * dvb:(hh + 1) * dvb] = y[off:off + nreal, :].astype(o_ref.dtype)
    for hh in range(hp):
        sout_ref[hh] = states[hh]


def gla(z, zr, wg, bgate, norm, s0, layer, *, row0, NB, T, H, dk, dvb, cols, blocks, mix=None, mix_col0=0):
    hp = 2 if H % 2 == 0 else 1
    cq, ck_, cv_, cg_ = cols
    wq, wv = hp * dk, hp * dvb
    assert cq % wq == 0 and ck_ % wq == 0 and cv_ % wv == 0 and cg_ % wv == 0 and mix_col0 % wv == 0
    rb = row0 // T
    kern = functools.partial(_gla_kernel, blocks=blocks, dk=dk, dvb=dvb, hp=hp)
    in_specs = [pl.BlockSpec((T, wq), lambda b, g: (rb + b, cq // wq + g)),
                pl.BlockSpec((T, wq), lambda b, g: (rb + b, ck_ // wq + g)),
                pl.BlockSpec((T, wv), lambda b, g: (rb + b, cv_ // wv + g)),
                pl.BlockSpec((T, wv), lambda b, g: (rb + b, cg_ // wv + g)),
                pl.BlockSpec((T, LANES), lambda b, g: (rb + b, 0)),
                pl.BlockSpec((None, LANES, wq), lambda b, g: (layer, 0, g)),
                pl.BlockSpec((None, 1, wq), lambda b, g: (layer, 0, g)),
                pl.BlockSpec((None, 1, dvb), lambda b, g: (layer, 0, 0)),
                pl.BlockSpec((None, hp, dk, dvb), lambda b, g: (b, g, 0, 0))]
    args = [z, z, z, z, zr, wg, bgate, norm, s0]
    aliases = {}
    if mix is None:
        y_shape = jax.ShapeDtypeStruct((NB * T, H * dvb), F32)
    else:
        y_shape = jax.ShapeDtypeStruct(mix.shape, mix.dtype)
        in_specs.append(pl.BlockSpec(memory_space=pl.ANY))
        args.append(mix)
        aliases = {len(args) - 1: 0}
    return pl.pallas_call(
        kern,
        out_shape=(y_shape, jax.ShapeDtypeStruct((NB, H, dk, dvb), F32)),
        grid=(NB, H // hp),
        in_specs=in_specs,
        out_specs=(pl.BlockSpec((T, wv), lambda b, g: (b, mix_col0 // wv + g)),
                   pl.BlockSpec((None, hp, dk, dvb), lambda b, g: (b, g, 0, 0))),
        input_output_aliases=aliases,
        compiler_params=_params(("parallel", "parallel")),
        name="gla",
    )(*args)


def _lru_kernel(cx_ref, cg_ref, buf_ref, h0_ref, cw_ref, cb_ref, wa_ref, ba_ref, wx_ref, bx_ref, lam_ref,
                *rest, T, cw, bw, conv_w):
    y_ref, hl_ref, nb_ref, e_ref, a_ref, x_ref = rest[-6:]
    seg = T // SUBLANES
    e_ref[0:SUBLANES, :] = buf_ref[...]
    e_ref[SUBLANES:SUBLANES + T, :] = cx_ref[...]
    first = SUBLANES - (conv_w - 1)
    u = cb_ref[...] + cw_ref[0:1, :] * e_ref[first:first + T, :]
    for j in range(1, conv_w):
        u = u + cw_ref[j:j + 1, :] * e_ref[first + j:first + j + T, :]
    nb_ref[...] = e_ref[T:T + SUBLANES, :]

    lam = lam_ref[...]
    sp = jnp.maximum(-lam, 0.0) + jnp.log1p(jnp.exp(-jnp.abs(lam)))
    for n in range(cw // bw):
        sl = slice(n * bw, (n + 1) * bw)
        un = u[:, sl]
        ub = un.astype(BF16)
        r = jax.nn.sigmoid(jnp.dot(ub, wa_ref[n].astype(BF16), preferred_element_type=F32) + ba_ref[:, sl])
        i = jax.nn.sigmoid(jnp.dot(ub, wx_ref[n].astype(BF16), preferred_element_type=F32) + bx_ref[:, sl])
        log_a = (-LRU_C) * r * sp[:, sl]
        a = jnp.exp(log_a)
        a_ref[n] = a
        y = (1.0 - a) * (1.0 + a)
        x_ref[n] = jnp.where(y > 0.0, y * lax.rsqrt(y), 0.0) * (i * un)

    nl = cw // bw
    rows = lambda t: pl.ds(t, SUBLANES, stride=seg)
    sub = lax.broadcasted_iota(jnp.int32, (SUBLANES, bw), 0)
    bcast = lambda v, s: jnp.broadcast_to(v[s:s + 1, :], (SUBLANES, bw))

    def pass1(t, carry):
        hs, ps = carry
        new_h, new_p = [], []
        for n in range(nl):
            a8 = a_ref[n, rows(t), :]
            h = a8 * hs[n] + x_ref[n, rows(t), :]
            p = a8 * ps[n]
            x_ref[n, rows(t), :] = h
            a_ref[n, rows(t), :] = p
            new_h.append(h)
            new_p.append(p)
        return tuple(new_h), tuple(new_p)

    h_init = tuple(jnp.where(sub == 0, jnp.broadcast_to(h0_ref[:, n * bw:(n + 1) * bw], (SUBLANES, bw)), 0.0)
                   for n in range(nl))
    p_init = tuple(jnp.ones((SUBLANES, bw), F32) for _ in range(nl))
    h_end, p_end = lax.fori_loop(0, seg, pass1, (h_init, p_init))

    carry_in = []
    for n in range(nl):
        e = bcast(h_end[n], 0)
        c = jnp.zeros((SUBLANES, bw), F32)
        for s in range(1, SUBLANES):
            c = jnp.where(sub == s, e, c)
            e = bcast(h_end[n], s) + bcast(p_end[n], s) * e
        carry_in.append(c)
        hl_ref[:, n * bw:(n + 1) * bw] = e[0:1, :]

    def pass2(t, _):
        for n in range(nl):
            x_ref[n, rows(t), :] = x_ref[n, rows(t), :] + a_ref[n, rows(t), :] * carry_in[n]
        return 0

    lax.fori_loop(0, seg, pass2, 0)
    for n in range(nl):
        sl = slice(n * bw, (n + 1) * bw)
        y_ref[:, sl] = (x_ref[n] * jax.nn.gelu(cg_ref[:, sl], approximate=True)).astype(y_ref.dtype)


def rglru(zc, bufp, h0, conv_w, conv_b, w_a, b_a, w_x, b_x, lam, layer, *, row0, NB, T, d_lru,
          mix=None, mix_col0=0):
    bw = w_a.shape[2]
    assert bw == LANES and T % SUBLANES == 0
    cw = _tile(d_lru, 512, bw)
    ncg = d_lru // cw
    nbk = cw // bw
    cwid = conv_w.shape[1]
    rb = row0 // T
    assert mix_col0 % cw == 0
    kern = functools.partial(_lru_kernel, T=T, cw=cw, bw=bw, conv_w=cwid)
    vec = lambda rows: pl.BlockSpec((None, rows, cw), lambda b, j: (layer, 0, j))
    in_specs = [pl.BlockSpec((T, cw), lambda b, j: (rb + b, j)),
                pl.BlockSpec((T, cw), lambda b, j: (rb + b, ncg + j)),
                pl.BlockSpec((None, SUBLANES, cw), lambda b, j: (b, 0, j)),
                pl.BlockSpec((None, 1, cw), lambda b, j: (b, 0, j)),
                vec(cwid), vec(1),
                pl.BlockSpec((None, nbk, bw, bw), lambda b, j: (layer, j, 0, 0)), vec(1),
                pl.BlockSpec((None, nbk, bw, bw), lambda b, j: (layer, j, 0, 0)), vec(1), vec(1)]
    args = [zc, zc, bufp, h0, conv_w, conv_b, w_a, b_a, w_x, b_x, lam]
    aliases = {}
    if mix is None:
        y_shape = jax.ShapeDtypeStruct((NB * T, d_lru), F32)
    else:
        y_shape = jax.ShapeDtypeStruct(mix.shape, mix.dtype)
        in_specs.append(pl.BlockSpec(memory_space=pl.ANY))
        args.append(mix)
        aliases = {len(args) - 1: 0}
    return pl.pallas_call(
        kern,
        out_shape=(y_shape,
                   jax.ShapeDtypeStruct((NB, 1, d_lru), F32),
                   jax.ShapeDtypeStruct((NB, SUBLANES, d_lru), F32)),
        grid=(NB, ncg),
        in_specs=in_specs,
        out_specs=(pl.BlockSpec((T, cw), lambda b, j: (b, mix_col0 // cw + j)),
                   pl.BlockSpec((None, 1, cw), lambda b, j: (b, 0, j)),
                   pl.BlockSpec((None, SUBLANES, cw), lambda b, j: (b, 0, j))),
        scratch_shapes=[pltpu.VMEM((T + SUBLANES, cw), F32), pltpu.VMEM((nbk, T, bw), F32),
                        pltpu.VMEM((nbk, T, bw), F32)],
        input_output_aliases=aliases,
        compiler_params=_params(("parallel", "parallel")),
        name="rglru",
    )(*args)


def kernel(x_prompt, x_sample, cache_k, cache_v, state_gla, state_lru_h, state_lru_conv, page_table, meta_tokens, norm_mix_pre, norm_mix_post, norm_mlp_pre, norm_mlp_post, w_in, w_out, lam_q1, lam_k1, lam_q2, lam_k2, attn_subln, gla_w_gate, gla_b_gate, gla_norm, lru_conv_w, lru_conv_b, lru_w_a, lru_b_a, lru_w_x, lru_b_x, lru_lambda, mlp_w_up, mlp_w_down):
    B, seq, D = x_prompt.shape
    DB, S, _ = x_sample.shape
    nm = meta_tokens.shape[0]
    T = nm + seq
    depth = w_in.shape[0]
    H_A, dv_a = cache_k.shape[3], cache_v.shape[4]
    dh_a = dv_a // 2
    d_a = H_A * dv_a
    H_B, dk_b, dv_b = state_gla.shape[2:]
    d_b = H_B * dv_b
    rank = gla_w_gate.shape[1]
    d_lru = state_lru_h.shape[-1]
    conv_w = lru_conv_w.shape[1]
    d_ff = mlp_w_up.shape[2]
    R_P, R_S = B * T, DB * S
    assert nm <= LANES and nm % 16 == 0 and seq % LANES == 0 and S <= SUBLANES and R_P % S == 0
    assert dv_a == LANES and dk_b == LANES and dv_b % LANES == 0 and rank <= LANES

    c_ak, c_av = d_a, 2 * d_a
    c_bq = 3 * d_a
    c_bk = c_bq + H_B * dk_b
    c_bv = c_bk + H_B * dk_b
    c_bg = c_bv + d_b
    c_br = c_bg + d_b
    c_cx = c_br + rank
    n_ab = c_br
    gla_cols = (c_bq, c_bk, c_bv, c_bg)
    d_mix = d_a + d_b + d_lru

    w_in_t = jnp.swapaxes(w_in, 1, 2)
    wg = jnp.pad(gla_w_gate, ((0, 0), (0, LANES - rank), (0, 0)))
    row3 = lambda a: a.reshape(depth, 1, a.shape[-1])

    slopes = jnp.exp2(-8.0 * jnp.arange(1, H_A + 1, dtype=F32) / H_A)
    slope_col = jnp.repeat(slopes, 2 * S).reshape(H_A * 2 * S, 1)
    eye_c = jnp.eye(2, dtype=F32)
    assert S * H_A <= LANES

    blocks_p = ((0, nm, LANES - nm),) + tuple((nm + LANES * n, LANES, 0) for n in range(seq // LANES))
    blocks_s = ((0, S, 0),)

    h = jnp.concatenate([piece for b in range(B) for piece in (meta_tokens, x_prompt[b])]
                        + [x_sample.reshape(R_S, D)], axis=0)
    zeros_gla = jnp.zeros((B, H_B, dk_b, dv_b), F32)
    zeros_h = jnp.zeros((B, 1, d_lru), F32)
    zeros_buf = jnp.zeros((B, SUBLANES, d_lru), F32)

    outs = [[] for _ in range(8)]
    kv_prompt = None
    hn = rms_cast(h, norm_mix_pre[0:1])
    for l in range(depth):
        lam_init = 0.8 - 0.6 * math.exp(-0.3 * l)
        lamp = jnp.stack([lam_q1[l], lam_k1[l], lam_q2[l], lam_k2[l]])
        subln = attn_subln[l:l + 1]

        tm = 2080
        z = matmul_nt(hn, w_in_t, l, 0, n_ab, tm_target=tm)
        zc = matmul_nt(hn, w_in_t, l, c_cx, 2 * d_lru, tm_target=tm)
        zr = matmul_nt(hn, w_in_t, l, c_br, rank)
        kv_prompt = kv_export(z, kv_prompt, l, depth=depth, B=B, T=T, H=H_A, dv=dv_a)

        mix = attn_prompt(z, slopes, lamp, subln, B=B, T=T, nm=nm, H=H_A, dh=dh_a, lam_init=lam_init,
                          mix_shape=(R_P + R_S, d_mix))
        q_s = z[R_P:, :d_a].reshape(DB, S, H_A, 2, dh_a) * (dh_a ** -0.5)
        qrows = jnp.einsum('bihcd,cy->bhciyd', q_s, eye_c).reshape(DB, H_A * 2 * S, dv_a).astype(BF16)
        new_rows = lambda c0: jnp.pad(z[R_P:, c0:c0 + d_a].reshape(DB, S * H_A, dv_a),
                                      ((0, 0), (0, LANES - S * H_A), (0, 0)))
        oa_s = attn_sample(qrows, new_rows(c_ak), new_rows(c_av), slope_col, lamp, subln, cache_k, cache_v,
                           page_table, l, DB=DB, S=S, H=H_A, dh=dh_a, lam_init=lam_init)
        gla_args = (wg, row3(gla_b_gate), row3(gla_norm))
        mix, sg_p = gla(z, zr, *gla_args, zeros_gla, l, row0=0, NB=B, T=T, H=H_B, dk=dk_b, dvb=dv_b,
                        cols=gla_cols, blocks=blocks_p, mix=mix, mix_col0=d_a)
        ob_s, sg_s = gla(z, zr, *gla_args, state_gla[l], l, row0=R_P, NB=DB, T=S, H=H_B, dk=dk_b, dvb=dv_b,
                         cols=gla_cols, blocks=blocks_s)
        lru_args = (lru_conv_w, row3(lru_conv_b), lru_w_a, row3(lru_b_a), lru_w_x, row3(lru_b_x),
                    row3(lru_lambda))
        mix, hl_p, nb_p = rglru(zc, zeros_buf, zeros_h, *lru_args, l, row0=0, NB=B, T=T, d_lru=d_lru,
                                mix=mix, mix_col0=d_a + d_b)
        bufp = jnp.pad(state_lru_conv[l], ((0, 0), (SUBLANES - (conv_w - 1), 0), (0, 0)))
        oc_s, hl_s, nb_s = rglru(zc, bufp, state_lru_h[l][:, None, :], *lru_args, l, row0=R_P, NB=DB, T=S,
                                 d_lru=d_lru)

        mix = lax.dynamic_update_slice(mix, jnp.concatenate([oa_s, ob_s, oc_s], axis=1).astype(BF16), (R_P, 0))
        m1 = matmul(mix, w_out, l, D, F32, tm_target=tm)
        h, hn2 = resid_norm(h, m1, norm_mix_post[l:l + 1], norm_mlp_pre[l:l + 1])
        up = matmul(hn2, mlp_w_up, l, d_ff, BF16, relu2=True, tm_target=tm)
        dn = matmul_ktiled(up, mlp_w_down, l)
        if l + 1 < depth:
            h, hn = resid_norm(h, dn, norm_mlp_post[l:l + 1], norm_mix_pre[l + 1:l + 2])

        new = (z[R_P:, c_ak:c_av].reshape(DB, S, H_A, dv_a), z[R_P:, c_av:c_bq].reshape(DB, S, H_A, dv_a),
               sg_p, sg_s, hl_p[:, 0, :], hl_s[:, 0, :],
               nb_p[:, SUBLANES - (conv_w - 1):, :], nb_s[:, SUBLANES - (conv_w - 1):, :])
        for acc, val in zip(outs, new):
            acc.append(val)

    gpost = norm_mlp_post[depth - 1:depth]
    y_prompt = resid_prompt_out(h, dn, gpost, B=B, T=T, nm=nm)
    y_sample = resid_norm(h[R_P:], dn[R_P:], gpost)[0].reshape(DB, S, D)
    k_prompt, v_prompt = (a.reshape(depth, B, T, H_A, dv_a) for a in kv_prompt)
    return (y_prompt, y_sample, k_prompt, v_prompt) + tuple(jnp.stack(o) for o in outs)
```

```python
import functools
import math

import jax
import jax.numpy as jnp
from jax import lax
from jax.experimental import pallas as pl
from jax.experimental.pallas import tpu as pltpu

F32 = jnp.float32
BF16 = jnp.bfloat16
EPS = 1e-6
GLA_TAU = 16.0
LRU_C = 8.0
NEG = -1e30
LANES = 128
SUBLANES = 8
VMEM_LIMIT_BYTES = 56 * 1024 * 1024

_NT = (((1,), (1,)), ((), ()))


def _tile(n, target, mult):
    best = None
    for d in range(mult, min(n, target) + 1, mult):
        if n % d == 0:
            best = d
    return n if best is None else best


def _params(sem):
    return pltpu.CompilerParams(dimension_semantics=sem, vmem_limit_bytes=VMEM_LIMIT_BYTES)


def _rms(x, g):
    return x * lax.rsqrt(jnp.mean(x * x, axis=-1, keepdims=True) + EPS) * g


def _rms_cast_kernel(x_ref, g_ref, o_ref):
    o_ref[...] = _rms(x_ref[...], g_ref[...]).astype(o_ref.dtype)


def rms_cast(x, g):
    R, D = x.shape
    tr = _tile(R, 416, 16)
    return pl.pallas_call(
        _rms_cast_kernel,
        out_shape=jax.ShapeDtypeStruct((R, D), BF16),
        grid=(R // tr,),
        in_specs=[pl.BlockSpec((tr, D), lambda i: (i, 0)), pl.BlockSpec((1, D), lambda i: (0, 0))],
        out_specs=pl.BlockSpec((tr, D), lambda i: (i, 0)),
        compiler_params=_params(("parallel",)),
        name="rms_cast",
    )(x, g)


def _resid_kernel(h_ref, m_ref, gpost_ref, o_ref):
    o_ref[...] = h_ref[...] + _rms(m_ref[...], gpost_ref[...])


def _resid_next_kernel(h_ref, m_ref, gpost_ref, gpre_ref, o_ref, hn_ref):
    h = h_ref[...] + _rms(m_ref[...], gpost_ref[...])
    o_ref[...] = h
    hn_ref[...] = _rms(h, gpre_ref[...]).astype(hn_ref.dtype)


def resid_norm(h, m, gpost, gpre=None):
    R, D = h.shape
    tr = _tile(R, 208, 16)
    row = pl.BlockSpec((tr, D), lambda i: (i, 0))
    vec = pl.BlockSpec((1, D), lambda i: (0, 0))
    if gpre is None:
        return pl.pallas_call(
            _resid_kernel, out_shape=jax.ShapeDtypeStruct((R, D), F32), grid=(R // tr,),
            in_specs=[row, row, vec], out_specs=row, compiler_params=_params(("parallel",)),
            name="resid_norm")(h, m, gpost), None
    return pl.pallas_call(
        _resid_next_kernel,
        out_shape=(jax.ShapeDtypeStruct((R, D), F32), jax.ShapeDtypeStruct((R, D), BF16)),
        grid=(R // tr,), in_specs=[row, row, vec, vec], out_specs=(row, row),
        compiler_params=_params(("parallel",)), name="resid_norm_next")(h, m, gpost, gpre)


def resid_prompt_out(h, m, gpost, *, B, T, nm):
    D = h.shape[1]
    seq = T - nm
    tr = _tile(seq, 256, SUBLANES)
    rows = pl.BlockSpec((pl.Element(tr), pl.Element(D)),
                        lambda b, i: (pl.multiple_of(b * T + nm + i * tr, SUBLANES), 0))
    return pl.pallas_call(
        _resid_kernel, out_shape=jax.ShapeDtypeStruct((B, seq, D), F32), grid=(B, seq // tr),
        in_specs=[rows, rows, pl.BlockSpec((1, D), lambda b, i: (0, 0))],
        out_specs=pl.BlockSpec((None, tr, D), lambda b, i: (b, i, 0)),
        compiler_params=_params(("parallel", "parallel")), name="resid_prompt_out")(h, m, gpost)


def _kv_export_kernel(*refs, depth, H, dv):
    ko_ref, vo_ref = refs[-2:]
    for l in range(depth):
        @pl.when(pl.program_id(0) == l)
        def _(k_ref=refs[2 * l], v_ref=refs[2 * l + 1]):
            tt = k_ref.shape[0]
            for h in range(H):
                ko_ref[pl.ds(h, tt, stride=H), :] = k_ref[:, h * dv:(h + 1) * dv]
                vo_ref[pl.ds(h, tt, stride=H), :] = v_ref[:, h * dv:(h + 1) * dv]


def kv_export(zs, *, B, T, H, dv):
    depth = len(zs)
    tt = _tile(T, 344, SUBLANES)
    nt = T // tt
    last = B * nt - 1
    out = jax.ShapeDtypeStruct((depth * B * T * H, dv), F32)

    def z_spec(l, col):
        def idx(L, b, i):
            return jnp.where(L == l, b * nt + i, jnp.where(L < l, 0, last)), col
        return pl.BlockSpec((tt, H * dv), idx)

    out_spec = pl.BlockSpec((tt * H, dv), lambda L, b, i: ((L * B + b) * nt + i, 0))
    return pl.pallas_call(
        functools.partial(_kv_export_kernel, depth=depth, H=H, dv=dv), out_shape=(out, out),
        grid=(depth, B, nt),
        in_specs=[z_spec(l, col) for l in range(depth) for col in (1, 2)],
        out_specs=(out_spec, out_spec),
        compiler_params=_params(("arbitrary", "arbitrary", "arbitrary")), name="kv_export",
    )(*[z for z in zs for _ in range(2)])


def _mm_kernel(x_ref, w_ref, o_ref, *, relu2):
    acc = jnp.dot(x_ref[...], w_ref[...].astype(BF16), preferred_element_type=F32)
    if relu2:
        acc = jnp.square(jnp.maximum(acc, 0.0))
    o_ref[...] = acc.astype(o_ref.dtype)


def _x_block(tm, K, tm_target):
    mode = dict(pipeline_mode=pl.Buffered(1)) if tm_target > 1040 else {}
    return pl.BlockSpec((tm, K), lambda i, j: (i, 0), **mode)


def matmul(x, w, layer, n_out, out_dtype, relu2=False, tm_target=1040, tn_target=512):
    R, K = x.shape
    tm = _tile(R, tm_target, 16)
    tn = _tile(n_out, tn_target, LANES)
    return pl.pallas_call(
        functools.partial(_mm_kernel, relu2=relu2),
        out_shape=jax.ShapeDtypeStruct((R, n_out), out_dtype),
        grid=(R // tm, n_out // tn),
        in_specs=[_x_block(tm, K, tm_target),
                  pl.BlockSpec((None, K, tn), lambda i, j: (layer, 0, j))],
        out_specs=pl.BlockSpec((tm, tn), lambda i, j: (i, j)),
        compiler_params=_params(("parallel", "parallel")),
        name="matmul",
    )(x, w)


def _mm_nt_kernel(x_ref, w_ref, o_ref):
    w = w_ref[0].astype(BF16)
    pad = o_ref.shape[1] - w.shape[0]
    if pad:
        w = jnp.concatenate([w, jnp.zeros((pad, w.shape[1]), BF16)], axis=0)
    o_ref[...] = lax.dot_general(x_ref[...], w, _NT, preferred_element_type=F32)


def matmul_nt(x, wt, layer, row0, n_rows, tm_target=1040, tn_target=512):
    R, K = x.shape
    assert row0 % SUBLANES == 0
    tm = _tile(R, tm_target, 16)
    tn = _tile(n_rows, tn_target, LANES) if n_rows >= LANES else n_rows
    n_out = max(n_rows, LANES)
    to = max(tn, LANES)
    return pl.pallas_call(
        _mm_nt_kernel,
        out_shape=jax.ShapeDtypeStruct((R, n_out), F32),
        grid=(R // tm, n_rows // tn),
        in_specs=[_x_block(tm, K, tm_target),
                  pl.BlockSpec((pl.Element(1), pl.Element(tn), pl.Element(K)),
                               lambda i, j: (layer, pl.multiple_of(row0 + j * tn, SUBLANES), 0))],
        out_specs=pl.BlockSpec((tm, to), lambda i, j: (i, j)),
        compiler_params=_params(("parallel", "parallel")),
        name="matmul_nt",
    )(x, wt)


def _mmk_kernel(x_ref, w_ref, o_ref):
    @pl.when(pl.program_id(2) == 0)
    def _():
        o_ref[...] = jnp.zeros_like(o_ref)

    o_ref[...] += jnp.dot(x_ref[...], w_ref[...].astype(BF16), preferred_element_type=F32)


def matmul_ktiled(x, w, layer, tm_target=2080, tn_target=1024, tk_target=1024):
    R, K = x.shape
    N = w.shape[2]
    tm = _tile(R, tm_target, 16)
    tn = _tile(N, tn_target, LANES)
    tk = _tile(K, tk_target, LANES)
    return pl.pallas_call(
        _mmk_kernel,
        out_shape=jax.ShapeDtypeStruct((R, N), F32),
        grid=(R // tm, N // tn, K // tk),
        in_specs=[pl.BlockSpec((tm, tk), lambda i, j, k: (i, k)),
                  pl.BlockSpec((None, tk, tn), lambda i, j, k: (layer, k, j))],
        out_specs=pl.BlockSpec((tm, tn), lambda i, j, k: (i, j)),
        compiler_params=_params(("parallel", "parallel", "arbitrary")),
        name="matmul_ktiled",
    )(x, w)


def _lam_value(lamp_ref, lam_init):
    lp = lamp_ref[...]
    a = jnp.exp(jnp.sum(lp[0:1] * lp[1:2], axis=-1, keepdims=True))
    b = jnp.exp(jnp.sum(lp[2:3] * lp[3:4], axis=-1, keepdims=True))
    return a - b + lam_init


def _attn_prompt_kernel(slopes_ref, lamp_ref, subln_ref, subln_col_ref, q_ref, k_ref, v_ref, o_ref,
                        kb_ref, vt_ref, *s_refs, nm, qb, nblk, dh, lam_init):
    T = q_ref.shape[0]
    dv = 2 * dh
    pad = LANES - nm
    log2e = math.log2(math.e)
    slope = slopes_ref[pl.program_id(1)] * log2e
    lam = _lam_value(lamp_ref, lam_init)
    scale = dh ** -0.5 * log2e

    lane = lax.broadcasted_iota(jnp.int32, (1, dv), 1)
    first_map = lane < dh

    kb_ref[0:pad, 0:dv] = jnp.zeros((pad, dv), BF16)
    kb_ref[pad:pad + T, 0:dv] = k_ref[...].astype(BF16)
    kbias = slope * (lax.broadcasted_iota(jnp.int32, (pad + T, dv), 0) - pad).astype(F32)
    hi = kbias.astype(BF16).astype(F32)
    mid = (kbias - hi).astype(BF16).astype(F32)
    lo = kbias - hi - mid
    kb_ref[:, dv:2 * dv] = jnp.where(lane == 0, hi, jnp.where(lane == 1, mid, jnp.where(lane == 2, lo, 0.0))
                                     ).astype(BF16)
    vt_ref[...] = jnp.concatenate([jnp.zeros((pad, dv), F32), v_ref[...]], axis=0).T.astype(BF16)
    ones3 = jnp.where(lane < 3, 1.0, 0.0)

    def stacked_q(q):
        qs = q * scale
        qq = jnp.concatenate([jnp.where(first_map, qs, 0.0), jnp.where(first_map, 0.0, qs)], axis=0)
        return jnp.concatenate([qq, jnp.broadcast_to(ones3, qq.shape)], axis=1).astype(BF16)

    qq = stacked_q(q_ref[0:nm, :])
    s = lax.dot_general(qq, kb_ref[0:LANES, :], _NT, preferred_element_type=F32)
    col = lax.broadcasted_iota(jnp.int32, (2 * nm, LANES), 1)
    row = lax.broadcasted_iota(jnp.int32, (2 * nm, LANES), 0)
    row = jnp.where(row >= nm, row - nm, row)
    kpos = col - pad
    s = jnp.where((kpos >= 0) & (kpos <= row), s, NEG)
    m = jnp.max(s, axis=-1, keepdims=True)
    p = jnp.exp2(s - m)
    l = jnp.sum(p, axis=-1, keepdims=True)
    o = lax.dot_general(p.astype(BF16), vt_ref[:, 0:LANES], _NT, preferred_element_type=F32) / l
    res = o[:nm] - lam * o[nm:]
    o_ref[0:nm, :] = (_rms(res, subln_ref[...]) * (1.0 - lam_init)).astype(o_ref.dtype)

    key_ok = lax.broadcasted_iota(jnp.int32, (LANES, qb), 0) >= pad
    causal = (lax.broadcasted_iota(jnp.int32, (qb, qb), 0)
              <= lax.broadcasted_iota(jnp.int32, (qb, qb), 1))
    nbuf = len(s_refs)
    units =[(qj, mp) for qj in range(nblk) for mp in range(2)]
    nkeys = lambda qj: LANES + (qj + 1) * qb
    qqs, outs = {}, {}

    def scores(u):
        qj, mp = units[u]
        r0, nk = nm + qj * qb, nkeys(qj)
        if mp == 0:
            qqs[qj] = stacked_q(q_ref[r0:r0 + qb, :])
        s = lax.dot_general(kb_ref[0:nk, :], qqs[qj][mp * qb:(mp + 1) * qb, :], _NT,
                            preferred_element_type=F32)
        parts = [jnp.where(key_ok, s[0:LANES, :], NEG)]
        if nk - qb > LANES:
            parts.append(s[LANES:nk - qb, :])
        parts.append(jnp.where(causal, s[nk - qb:nk, :], NEG))
        s_refs[u % nbuf][0:nk, :] = jnp.concatenate(parts, axis=0)

    def softmax_values(u):
        qj, mp = units[u]
        r0, nk = nm + qj * qb, nkeys(qj)
        s_ref = s_refs[u % nbuf]
        m = jnp.max(s_ref[0:nk, :], axis=0, keepdims=True)
        p = jnp.exp2(s_ref[0:nk, :] - m)
        l = jnp.sum(p, axis=0, keepdims=True)
        outs[mp] = jnp.dot(vt_ref[:, 0:nk], p.astype(BF16), preferred_element_type=F32) / l
        if mp == 1:
            res = outs[0] - lam * outs[1]
            y = res * lax.rsqrt(jnp.mean(res * res, axis=0, keepdims=True) + EPS) * subln_col_ref[...]
            o_ref[r0:r0 + qb, :] = (y * (1.0 - lam_init)).T.astype(o_ref.dtype)

    scores(0)
    for u in range(1, len(units)):
        scores(u)
        softmax_values(u - 1)
    softmax_values(len(units) - 1)


def attn_prompt(z, slopes, lamp, subln, *, B, T, nm, H, dh, lam_init, mix_shape):
    seq = T - nm
    qb = _tile(seq, 256, LANES)
    nblk = seq // qb
    dv = 2 * dh
    kern = functools.partial(_attn_prompt_kernel, nm=nm, qb=qb, nblk=nblk, dh=dh, lam_init=lam_init)
    return pl.pallas_call(
        kern,
        out_shape=jax.ShapeDtypeStruct(mix_shape, BF16),
        grid=(B, H),
        in_specs=[pl.BlockSpec(memory_space=pltpu.SMEM),
                  pl.BlockSpec((4, dh), lambda b, h: (0, 0)),
                  pl.BlockSpec((1, dv), lambda b, h: (0, 0)),
                  pl.BlockSpec((dv, 1), lambda b, h: (0, 0)),
                  pl.BlockSpec((T, dv), lambda b, h: (b, h)),
                  pl.BlockSpec((T, dv), lambda b, h: (b, H + h)),
                  pl.BlockSpec((T, dv), lambda b, h: (b, 2 * H + h))],
        out_specs=pl.BlockSpec((T, dv), lambda b, h: (b, h)),
        scratch_shapes=[pltpu.VMEM((LANES + seq, 2 * dv), BF16), pltpu.VMEM((dv, LANES + seq), BF16),
                        ] + [pltpu.VMEM((LANES + seq, qb), F32)] * 3,
        compiler_params=_params(("parallel", "parallel")),
        name="attn_prompt",
    )(slopes, lamp, subln, subln.reshape(dv, 1), z, z, z)


def _attn_sample_kernel(pt_ref, lamp_ref, subln_ref, slope_ref, q_ref, kn_ref, vn_ref, *rest,
                        G, page, past_len, H, S, dv, lam_init):
    k_refs, v_refs = rest[:G], rest[G:2 * G]
    o_ref, m_ref, l_ref, acc_ref, bias_ref = rest[2 * G:]
    j = pl.program_id(1)
    nrow = H * 2 * S
    h_shift, s_shift = H.bit_length() - 1, (2 * S).bit_length() - 1
    row = lax.broadcasted_iota(jnp.int32, (nrow, 1), 0)
    row_head = lax.shift_right_logical(row, s_shift)
    slope = slope_ref[...]

    def block_bias(n, limit=None):
        col = lax.broadcasted_iota(jnp.int32, (1, n), 1)
        key = lax.shift_right_logical(col, h_shift)
        ok = (col & (H - 1)) == row_head
        if limit is not None:
            ok = ok & (key <= limit)
        return jnp.where(ok, slope * key.astype(F32), NEG)

    @pl.when(j == 0)
    def _():
        m_ref[...] = jnp.full(m_ref.shape, NEG, F32)
        l_ref[...] = jnp.zeros(l_ref.shape, F32)
        acc_ref[...] = jnp.zeros(acc_ref.shape, F32)
        bias_ref[...] = block_bias(bias_ref.shape[1])

    def update(k_parts, v_parts, bias, shift):
        q = q_ref[...]
        s = jnp.concatenate([lax.dot_general(q, r[...].astype(BF16), _NT, preferred_element_type=F32)
                             for r in k_parts], axis=1) + bias
        m_old = m_ref[...]
        m_new = jnp.maximum(m_old, jnp.max(s, axis=-1, keepdims=True) + shift)
        alpha = jnp.exp2(m_old - m_new)
        p = jnp.exp2(s - (m_new - shift))
        l_ref[...] = alpha * l_ref[...] + jnp.sum(p, axis=-1, keepdims=True)
        p = p.astype(BF16)
        pv, c0 = None, 0
        for r in v_parts:
            d = jnp.dot(p[:, c0:c0 + r.shape[0]], r[...].astype(BF16), preferred_element_type=F32)
            pv = d if pv is None else pv + d
            c0 += r.shape[0]
        acc_ref[...] = alpha * acc_ref[...] + pv
        m_ref[...] = m_new

    update(k_refs, v_refs, bias_ref[...], slope * (j * (G * page) - past_len).astype(F32))

    @pl.when(j == pl.num_programs(1) - 1)
    def _():
        update([kn_ref], [vn_ref], block_bias(kn_ref.shape[0], row & (S - 1)), 0.0)
        lam = _lam_value(lamp_ref, lam_init)
        o = acc_ref[...] / l_ref[...]
        for h in range(H):
            blk = o[h * 2 * S:(h + 1) * 2 * S, :]
            res = blk[:S] - lam * blk[S:]
            o_ref[:, h * dv:(h + 1) * dv] = _rms(res, subln_ref[...]) * (1.0 - lam_init)


def attn_sample(qrows, knew, vnew, slope_col, lamp, subln, cache_k, cache_v, page_table, layer, *,
                DB, S, H, dh, lam_init):
    dv = 2 * dh
    n_pages = page_table.shape[1]
    page = cache_k.shape[2]
    assert H & (H - 1) == 0 and S & (S - 1) == 0
    G = _tile(n_pages, 8, 1)
    ck = cache_k.reshape(cache_k.shape[0], cache_k.shape[1], page * H, dv)
    cv = cache_v.reshape(cache_v.shape[0], cache_v.shape[1], page * H, dv)
    nrow = H * 2 * S

    def page_spec(g):
        return pl.BlockSpec((None, None, page * H, dv), lambda b, j, pt: (layer, pt[b, j * G + g], 0, 0))

    kern = functools.partial(_attn_sample_kernel, G=G, page=page, past_len=n_pages * page,
                             H=H, S=S, dv=dv, lam_init=lam_init)
    per_b = lambda rows: pl.BlockSpec((None, rows, dv), lambda b, j, pt: (b, 0, 0))
    grid_spec = pltpu.PrefetchScalarGridSpec(
        num_scalar_prefetch=1,
        grid=(DB, n_pages // G),
        in_specs=[pl.BlockSpec((4, dh), lambda b, j, pt: (0, 0)),
                  pl.BlockSpec((1, dv), lambda b, j, pt: (0, 0)),
                  pl.BlockSpec((nrow, 1), lambda b, j, pt: (0, 0)),
                  per_b(nrow), per_b(knew.shape[1]), per_b(vnew.shape[1])]
                 + [page_spec(g) for g in range(G)] + [page_spec(g) for g in range(G)],
        out_specs=pl.BlockSpec((S, H * dv), lambda b, j, pt: (b, 0)),
        scratch_shapes=[pltpu.VMEM((nrow, 1), F32), pltpu.VMEM((nrow, 1), F32),
                        pltpu.VMEM((nrow, dv), F32), pltpu.VMEM((nrow, G * page * H), F32)],
    )
    return pl.pallas_call(
        kern, out_shape=jax.ShapeDtypeStruct((DB * S, H * dv), F32), grid_spec=grid_spec,
        compiler_params=_params(("parallel", "arbitrary")), name="attn_sample",
    )(page_table, lamp, subln, slope_col, qrows, knew, vnew, *([ck] * G), *([cv] * G))


def _gla_kernel(q_ref, k_ref, v_ref, g_ref, r_ref, wg_ref, bg_ref, norm_ref, s0_ref, *rest,
                blocks, dk, dvb, hp):
    o_ref, sout_ref = rest[-2:]
    L = LANES
    ri = lax.broadcasted_iota(jnp.int32, (L, L), 0)
    ci = lax.broadcasted_iota(jnp.int32, (L, L), 1)
    tril = ri >= ci
    tril_bf = jnp.where(tril, 1.0, 0.0).astype(BF16)
    rowi = lax.broadcasted_iota(jnp.int32, (L, 1), 0)
    qscale = dk ** -0.5
    states = [s0_ref[hh] for hh in range(hp)]

    def load(ref, row0, nreal, off, c0, width):
        x = ref[row0:row0 + nreal, c0:c0 + width]
        if nreal == L:
            return x
        parts = []
        if off:
            parts.append(jnp.zeros((off, width), x.dtype))
        parts.append(x)
        if L - off - nreal:
            parts.append(jnp.zeros((L - off - nreal, width), x.dtype))
        return jnp.concatenate(parts, axis=0)

    r_all = jnp.concatenate([load(r_ref, *blk, 0, L) for blk in blocks], axis=0).astype(BF16)
    x_all = jnp.dot(r_all, wg_ref[...].astype(BF16), preferred_element_type=F32) + bg_ref[...]
    gk_all = (jnp.minimum(x_all, 0.0) - jnp.log1p(jnp.exp(-jnp.abs(x_all)))) * (1.0 / GLA_TAU)
    pieces = []
    for n, (row0, nreal, off) in enumerate(blocks):
        g_n = gk_all[n * L:(n + 1) * L, :]
        if nreal < L:
            g_n = jnp.where((rowi >= off) & (rowi < off + nreal), g_n, 0.0)
        pieces.append(g_n)
    g_wide = jnp.concatenate(pieces, axis=1)
    hi = g_wide.astype(BF16)
    lo = (g_wide - hi.astype(F32)).astype(BF16)
    b_wide = (jnp.dot(tril_bf, hi, preferred_element_type=F32)
              + jnp.dot(tril_bf, lo, preferred_element_type=F32))

    for n, (row0, nreal, off) in enumerate(blocks):
        for hh in range(hp):
            S = states[hh]
            q = load(q_ref, row0, nreal, off, hh * dk, dk) * qscale
            k = load(k_ref, row0, nreal, off, hh * dk, dk)
            v = load(v_ref, row0, nreal, off, hh * dvb, dvb).astype(BF16)
            b = b_wide[:, (n * hp + hh) * dk:(n * hp + hh + 1) * dk]
            b_last = b[L - 1:L, :]
            b_mid = b[L // 2 - 1:L // 2, :]
            qs = (q * jnp.exp(b)).astype(BF16)
            qm = (q * jnp.exp(b - b_mid)).astype(BF16)
            km = (k * jnp.exp(b_mid - b)).astype(BF16)
            att = lax.dot_general(qm, km, _NT, preferred_element_type=F32)
            att = jnp.where(tril, att, 0.0).astype(BF16)
            o = jnp.dot(jnp.concatenate([att, qs], axis=1),
                        jnp.concatenate([v, S.astype(BF16)], axis=0), preferred_element_type=F32)
            kk = (k * jnp.exp(b_last - b)).T.astype(BF16)
            decay = jnp.broadcast_to(jnp.exp(b_last), (L, dk)).T
            states[hh] = (jnp.concatenate([decay] * (dvb // L), axis=1) * S
                          + jnp.dot(kk, v, preferred_element_type=F32))
            y = _rms(o, norm_ref[...])
            g = load(g_ref, row0, nreal, off, hh * dvb, dvb)
            y = y * (g / (1.0 + jnp.exp(-g)))
            o_ref[row0:row0 + nreal, hh * dvb:(hh + 1) * dvb] = y[off:off + nreal, :].astype(o_ref.dtype)
    for hh in range(hp):
        sout_ref[hh] = states[hh]


def gla(z, zr, wg, bgate, norm, s0, layer, *, row0, NB, T, H, dk, dvb, cols, blocks, mix=None, mix_col0=0):
    hp = 2 if H % 2 == 0 else 1
    cq, ck_, cv_, cg_ = cols
    wq, wv = hp * dk, hp * dvb
    assert cq % wq == 0 and ck_ % wq == 0 and cv_ % wv == 0 and cg_ % wv == 0 and mix_col0 % wv == 0
    rb = row0 // T
    kern = functools.partial(_gla_kernel, blocks=blocks, dk=dk, dvb=dvb, hp=hp)
    in_specs = [pl.BlockSpec((T, wq), lambda b, g: (rb + b, cq // wq + g)),
                pl.BlockSpec((T, wq), lambda b, g: (rb + b, ck_ // wq + g)),
                pl.BlockSpec((T, wv), lambda b, g: (rb + b, cv_ // wv + g)),
                pl.BlockSpec((T, wv), lambda b, g: (rb + b, cg_ // wv + g)),
                pl.BlockSpec((T, LANES), lambda b, g: (rb + b, 0)),
                pl.BlockSpec((None, LANES, wq), lambda b, g: (layer, 0, g)),
                pl.BlockSpec((None, 1, wq), lambda b, g: (layer, 0, g)),
                pl.BlockSpec((None, 1, dvb), lambda b, g: (layer, 0, 0)),
                pl.BlockSpec((None, hp, dk, dvb), lambda b, g: (b, g, 0, 0))]
    args = [z, z, z, z, zr, wg, bgate, norm, s0]
    aliases = {}
    if mix is None:
        y_shape = jax.ShapeDtypeStruct((NB * T, H * dvb), F32)
    else:
        y_shape = jax.ShapeDtypeStruct(mix.shape, mix.dtype)
        in_specs.append(pl.BlockSpec(memory_space=pl.ANY))
        args.append(mix)
        aliases = {len(args) - 1: 0}
    return pl.pallas_call(
        kern,
        out_shape=(y_shape, jax.ShapeDtypeStruct((NB, H, dk, dvb), F32)),
        grid=(NB, H // hp),
        in_specs=in_specs,
        out_specs=(pl.BlockSpec((T, wv), lambda b, g: (b, mix_col0 // wv + g)),
                   pl.BlockSpec((None, hp, dk, dvb), lambda b, g: (b, g, 0, 0))),
        input_output_aliases=aliases,
        compiler_params=_params(("parallel", "parallel")),
        name="gla",
    )(*args)


def _lru_kernel(cx_ref, cg_ref, buf_ref, h0_ref, cw_ref, cb_ref, wa_ref, ba_ref, wx_ref, bx_ref, lam_ref,
                *rest, T, cw, bw, conv_w):
    y_ref, hl_ref, nb_ref, e_ref, a_ref, x_ref = rest[-6:]
    seg = T // SUBLANES
    e_ref[0:SUBLANES, :] = buf_ref[...]
    e_ref[SUBLANES:SUBLANES + T, :] = cx_ref[...]
    first = SUBLANES - (conv_w - 1)
    u = cb_ref[...] + cw_ref[0:1, :] * e_ref[first:first + T, :]
    for j in range(1, conv_w):
        u = u + cw_ref[j:j + 1, :] * e_ref[first + j:first + j + T, :]
    nb_ref[...] = e_ref[T:T + SUBLANES, :]

    lam = lam_ref[...]
    sp = jnp.maximum(-lam, 0.0) + jnp.log1p(jnp.exp(-jnp.abs(lam)))
    for n in range(cw // bw):
        sl = slice(n * bw, (n + 1) * bw)
        un = u[:, sl]
        ub = un.astype(BF16)
        r = jax.nn.sigmoid(jnp.dot(ub, wa_ref[n].astype(BF16), preferred_element_type=F32) + ba_ref[:, sl])
        i = jax.nn.sigmoid(jnp.dot(ub, wx_ref[n].astype(BF16), preferred_element_type=F32) + bx_ref[:, sl])
        log_a = (-LRU_C) * r * sp[:, sl]
        a = jnp.exp(log_a)
        a_ref[n] = a
        y = (1.0 - a) * (1.0 + a)
        x_ref[n] = jnp.where(y > 0.0, y * lax.rsqrt(y), 0.0) * (i * un)

    nl = cw // bw
    rows = lambda t: pl.ds(t, SUBLANES, stride=seg)
    sub = lax.broadcasted_iota(jnp.int32, (SUBLANES, bw), 0)
    bcast = lambda v, s: jnp.broadcast_to(v[s:s + 1, :], (SUBLANES, bw))

    def pass1(t, carry):
        hs, ps = carry
        new_h, new_p = [], []
        for n in range(nl):
            a8 = a_ref[n, rows(t), :]
            h = a8 * hs[n] + x_ref[n, rows(t), :]
            p = a8 * ps[n]
            x_ref[n, rows(t), :] = h
            a_ref[n, rows(t), :] = p
            new_h.append(h)
            new_p.append(p)
        return tuple(new_h), tuple(new_p)

    h_init = tuple(jnp.where(sub == 0, jnp.broadcast_to(h0_ref[:, n * bw:(n + 1) * bw], (SUBLANES, bw)), 0.0)
                   for n in range(nl))
    p_init = tuple(jnp.ones((SUBLANES, bw), F32) for _ in range(nl))
    h_end, p_end = lax.fori_loop(0, seg, pass1, (h_init, p_init))

    carry_in = []
    for n in range(nl):
        e = bcast(h_end[n], 0)
        c = jnp.zeros((SUBLANES, bw), F32)
        for s in range(1, SUBLANES):
            c = jnp.where(sub == s, e, c)
            e = bcast(h_end[n], s) + bcast(p_end[n], s) * e
        carry_in.append(c)
        hl_ref[:, n * bw:(n + 1) * bw] = e[0:1, :]

    def pass2(t, _):
        for n in range(nl):
            x_ref[n, rows(t), :] = x_ref[n, rows(t), :] + a_ref[n, rows(t), :] * carry_in[n]
        return 0

    lax.fori_loop(0, seg, pass2, 0)
    for n in range(nl):
        sl = slice(n * bw, (n + 1) * bw)
        y_ref[:, sl] = (x_ref[n] * jax.nn.gelu(cg_ref[:, sl], approximate=True)).astype(y_ref.dtype)


def rglru(zc, bufp, h0, conv_w, conv_b, w_a, b_a, w_x, b_x, lam, layer, *, row0, NB, T, d_lru,
          mix=None, mix_col0=0):
    bw = w_a.shape[2]
    assert bw == LANES and T % SUBLANES == 0
    cw = _tile(d_lru, 512, bw)
    ncg = d_lru // cw
    nbk = cw // bw
    cwid = conv_w.shape[1]
    rb = row0 // T
    assert mix_col0 % cw == 0
    kern = functools.partial(_lru_kernel, T=T, cw=cw, bw=bw, conv_w=cwid)
    vec = lambda rows: pl.BlockSpec((None, rows, cw), lambda b, j: (layer, 0, j))
    in_specs = [pl.BlockSpec((T, cw), lambda b, j: (rb + b, j)),
                pl.BlockSpec((T, cw), lambda b, j: (rb + b, ncg + j)),
                pl.BlockSpec((None, SUBLANES, cw), lambda b, j: (b, 0, j)),
                pl.BlockSpec((None, 1, cw), lambda b, j: (b, 0, j)),
                vec(cwid), vec(1),
                pl.BlockSpec((None, nbk, bw, bw), lambda b, j: (layer, j, 0, 0)), vec(1),
                pl.BlockSpec((None, nbk, bw, bw), lambda b, j: (layer, j, 0, 0)), vec(1), vec(1)]
    args = [zc, zc, bufp, h0, conv_w, conv_b, w_a, b_a, w_x, b_x, lam]
    aliases = {}
    if mix is None:
        y_shape = jax.ShapeDtypeStruct((NB * T, d_lru), F32)
    else:
        y_shape = jax.ShapeDtypeStruct(mix.shape, mix.dtype)
        in_specs.append(pl.BlockSpec(memory_space=pl.ANY))
        args.append(mix)
        aliases = {len(args) - 1: 0}
    return pl.pallas_call(
        kern,
        out_shape=(y_shape,
                   jax.ShapeDtypeStruct((NB, 1, d_lru), F32),
                   jax.ShapeDtypeStruct((NB, SUBLANES, d_lru), F32)),
        grid=(NB, ncg),
        in_specs=in_specs,
        out_specs=(pl.BlockSpec((T, cw), lambda b, j: (b, mix_col0 // cw + j)),
                   pl.BlockSpec((None, 1, cw), lambda b, j: (b, 0, j)),
                   pl.BlockSpec((None, SUBLANES, cw), lambda b, j: (b, 0, j))),
        scratch_shapes=[pltpu.VMEM((T + SUBLANES, cw), F32), pltpu.VMEM((nbk, T, bw), F32),
                        pltpu.VMEM((nbk, T, bw), F32)],
        input_output_aliases=aliases,
        compiler_params=_params(("parallel", "parallel")),
        name="rglru",
    )(*args)


def kernel(x_prompt, x_sample, cache_k, cache_v, state_gla, state_lru_h, state_lru_conv, page_table, meta_tokens, norm_mix_pre, norm_mix_post, norm_mlp_pre, norm_mlp_post, w_in, w_out, lam_q1, lam_k1, lam_q2, lam_k2, attn_subln, gla_w_gate, gla_b_gate, gla_norm, lru_conv_w, lru_conv_b, lru_w_a, lru_b_a, lru_w_x, lru_b_x, lru_lambda, mlp_w_up, mlp_w_down):
    B, seq, D = x_prompt.shape
    DB, S, _ = x_sample.shape
    nm = meta_tokens.shape[0]
    T = nm + seq
    depth = w_in.shape[0]
    H_A, dv_a = cache_k.shape[3], cache_v.shape[4]
    dh_a = dv_a // 2
    d_a = H_A * dv_a
    H_B, dk_b, dv_b = state_gla.shape[2:]
    d_b = H_B * dv_b
    rank = gla_w_gate.shape[1]
    d_lru = state_lru_h.shape[-1]
    conv_w = lru_conv_w.shape[1]
    d_ff = mlp_w_up.shape[2]
    R_P, R_S = B * T, DB * S
    assert nm <= LANES and nm % 16 == 0 and seq % LANES == 0 and S <= SUBLANES and R_P % S == 0
    assert dv_a == LANES and dk_b == LANES and dv_b % LANES == 0 and rank <= LANES

    c_ak, c_av = d_a, 2 * d_a
    c_bq = 3 * d_a
    c_bk = c_bq + H_B * dk_b
    c_bv = c_bk + H_B * dk_b
    c_bg = c_bv + d_b
    c_br = c_bg + d_b
    c_cx = c_br + rank
    n_ab = c_br
    gla_cols = (c_bq, c_bk, c_bv, c_bg)
    d_mix = d_a + d_b + d_lru

    w_in_t = jnp.swapaxes(w_in, 1, 2)
    wg = jnp.pad(gla_w_gate, ((0, 0), (0, LANES - rank), (0, 0)))
    row3 = lambda a: a.reshape(depth, 1, a.shape[-1])

    slopes = jnp.exp2(-8.0 * jnp.arange(1, H_A + 1, dtype=F32) / H_A)
    log2e = math.log2(math.e)
    slope_col = jnp.repeat(slopes * log2e, 2 * S).reshape(H_A * 2 * S, 1)
    eye_c = jnp.eye(2, dtype=F32)
    assert S * H_A <= LANES

    blocks_p = ((0, nm, LANES - nm),) + tuple((nm + LANES * n, LANES, 0) for n in range(seq // LANES))
    blocks_s = ((0, S, 0),)

    h = jnp.concatenate([piece for b in range(B) for piece in (meta_tokens, x_prompt[b])]
                        + [x_sample.reshape(R_S, D)], axis=0)
    zeros_gla = jnp.zeros((B, H_B, dk_b, dv_b), F32)
    zeros_h = jnp.zeros((B, 1, d_lru), F32)
    zeros_buf = jnp.zeros((B, SUBLANES, d_lru), F32)

    outs = [[] for _ in range(8)]
    zs = []
    hn = rms_cast(h, norm_mix_pre[0:1])
    for l in range(depth):
        lam_init = 0.8 - 0.6 * math.exp(-0.3 * l)
        lamp = jnp.stack([lam_q1[l], lam_k1[l], lam_q2[l], lam_k2[l]])
        subln = attn_subln[l:l + 1]

        tm = 2080
        z = matmul_nt(hn, w_in_t, l, 0, n_ab, tm_target=tm)
        zc = matmul_nt(hn, w_in_t, l, c_cx, 2 * d_lru, tm_target=tm)
        zr = matmul_nt(hn, w_in_t, l, c_br, rank)
        zs.append(z)

        mix = attn_prompt(z, slopes, lamp, subln, B=B, T=T, nm=nm, H=H_A, dh=dh_a, lam_init=lam_init,
                          mix_shape=(R_P + R_S, d_mix))
        q_s = z[R_P:, :d_a].reshape(DB, S, H_A, 2, dh_a) * (dh_a ** -0.5 * log2e)
        qrows = jnp.einsum('bihcd,cy->bhciyd', q_s, eye_c).reshape(DB, H_A * 2 * S, dv_a).astype(BF16)
        new_rows = lambda c0: jnp.pad(z[R_P:, c0:c0 + d_a].reshape(DB, S * H_A, dv_a),
                                      ((0, 0), (0, LANES - S * H_A), (0, 0)))
        oa_s = attn_sample(qrows, new_rows(c_ak), new_rows(c_av), slope_col, lamp, subln, cache_k, cache_v,
                           page_table, l, DB=DB, S=S, H=H_A, dh=dh_a, lam_init=lam_init)
        gla_args = (wg, row3(gla_b_gate), row3(gla_norm))
        mix, sg_p = gla(z, zr, *gla_args, zeros_gla, l, row0=0, NB=B, T=T, H=H_B, dk=dk_b, dvb=dv_b,
                        cols=gla_cols, blocks=blocks_p, mix=mix, mix_col0=d_a)
        ob_s, sg_s = gla(z, zr, *gla_args, state_gla[l], l, row0=R_P, NB=DB, T=S, H=H_B, dk=dk_b, dvb=dv_b,
                         cols=gla_cols, blocks=blocks_s)
        lru_args = (lru_conv_w, row3(lru_conv_b), lru_w_a, row3(lru_b_a), lru_w_x, row3(lru_b_x),
                    row3(lru_lambda))
        mix, hl_p, nb_p = rglru(zc, zeros_buf, zeros_h, *lru_args, l, row0=0, NB=B, T=T, d_lru=d_lru,
                                mix=mix, mix_col0=d_a + d_b)
        bufp = jnp.pad(state_lru_conv[l], ((0, 0), (SUBLANES - (conv_w - 1), 0), (0, 0)))
        oc_s, hl_s, nb_s = rglru(zc, bufp, state_lru_h[l][:, None, :], *lru_args, l, row0=R_P, NB=DB, T=S,
                                 d_lru=d_lru)

        mix = lax.dynamic_update_slice(mix, jnp.concatenate([oa_s, ob_s, oc_s], axis=1).astype(BF16), (R_P, 0))
        m1 = matmul(mix, w_out, l, D, F32, tm_target=tm)
        h, hn2 = resid_norm(h, m1, norm_mix_post[l:l + 1], norm_mlp_pre[l:l + 1])
        up = matmul(hn2, mlp_w_up, l, d_ff, BF16, relu2=True, tm_target=tm)
        dn = matmul_ktiled(up, mlp_w_down, l)
        if l + 1 < depth:
            h, hn = resid_norm(h, dn, norm_mlp_post[l:l + 1], norm_mix_pre[l + 1:l + 2])

        new = (z[R_P:, c_ak:c_av].reshape(DB, S, H_A, dv_a), z[R_P:, c_av:c_bq].reshape(DB, S, H_A, dv_a),
               sg_p, sg_s, hl_p[:, 0, :], hl_s[:, 0, :],
               nb_p[:, SUBLANES - (conv_w - 1):, :], nb_s[:, SUBLANES - (conv_w - 1):, :])
        for acc, val in zip(outs, new):
            acc.append(val)

    gpost = norm_mlp_post[depth - 1:depth]
    y_prompt = resid_prompt_out(h, dn, gpost, B=B, T=T, nm=nm)
    y_sample = resid_norm(h[R_P:], dn[R_P:], gpost)[0].reshape(DB, S, D)
    k_prompt, v_prompt = (a.reshape(depth, B, T, H_A, dv_a) for a in kv_export(zs, B=B, T=T, H=H_A, dv=dv_a))
    return (y_prompt, y_sample, k_prompt, v_prompt) + tuple(jnp.stack(o) for o in outs)
```

```python
import functools
import math

import jax
import jax.numpy as jnp
from jax import lax
from jax.experimental import pallas as pl
from jax.experimental.pallas import tpu as pltpu

F32 = jnp.float32
BF16 = jnp.bfloat16
EPS = 1e-6
GLA_TAU = 16.0
LRU_C = 8.0
NEG = -1e30
LANES = 128
SUBLANES = 8
VMEM_LIMIT_BYTES = 56 * 1024 * 1024

_NT = (((1,), (1,)), ((), ()))


def _tile(n, target, mult):
    best = None
    for d in range(mult, min(n, target) + 1, mult):
        if n % d == 0:
            best = d
    return n if best is None else best


def _params(sem):
    return pltpu.CompilerParams(dimension_semantics=sem, vmem_limit_bytes=VMEM_LIMIT_BYTES)


def _rms(x, g):
    return x * lax.rsqrt(jnp.mean(x * x, axis=-1, keepdims=True) + EPS) * g


def _rms_cast_kernel(x_ref, g_ref, o_ref):
    o_ref[...] = _rms(x_ref[...], g_ref[...]).astype(o_ref.dtype)


def rms_cast(x, g):
    R, D = x.shape
    tr = _tile(R, 416, 16)
    return pl.pallas_call(
        _rms_cast_kernel,
        out_shape=jax.ShapeDtypeStruct((R, D), BF16),
        grid=(R // tr,),
        in_specs=[pl.BlockSpec((tr, D), lambda i: (i, 0)), pl.BlockSpec((1, D), lambda i: (0, 0))],
        out_specs=pl.BlockSpec((tr, D), lambda i: (i, 0)),
        compiler_params=_params(("parallel",)),
        name="rms_cast",
    )(x, g)


def _resid_kernel(h_ref, m_ref, gpost_ref, o_ref):
    o_ref[...] = h_ref[...] + _rms(m_ref[...], gpost_ref[...])


def _resid_next_kernel(h_ref, m_ref, gpost_ref, gpre_ref, o_ref, hn_ref):
    h = h_ref[...] + _rms(m_ref[...], gpost_ref[...])
    o_ref[...] = h
    hn_ref[...] = _rms(h, gpre_ref[...]).astype(hn_ref.dtype)


def resid_norm(h, m, gpost, gpre=None):
    R, D = h.shape
    tr = _tile(R, 208, 16)
    row = pl.BlockSpec((tr, D), lambda i: (i, 0))
    vec = pl.BlockSpec((1, D), lambda i: (0, 0))
    if gpre is None:
        return pl.pallas_call(
            _resid_kernel, out_shape=jax.ShapeDtypeStruct((R, D), F32), grid=(R // tr,),
            in_specs=[row, row, vec], out_specs=row, compiler_params=_params(("parallel",)),
            name="resid_norm")(h, m, gpost), None
    return pl.pallas_call(
        _resid_next_kernel,
        out_shape=(jax.ShapeDtypeStruct((R, D), F32), jax.ShapeDtypeStruct((R, D), BF16)),
        grid=(R // tr,), in_specs=[row, row, vec, vec], out_specs=(row, row),
        compiler_params=_params(("parallel",)), name="resid_norm_next")(h, m, gpost, gpre)


def resid_prompt_out(h, m, gpost, *, B, T, nm):
    D = h.shape[1]
    seq = T - nm
    tr = _tile(seq, 256, SUBLANES)
    rows = pl.BlockSpec((pl.Element(tr), pl.Element(D)),
                        lambda b, i: (pl.multiple_of(b * T + nm + i * tr, SUBLANES), 0))
    return pl.pallas_call(
        _resid_kernel, out_shape=jax.ShapeDtypeStruct((B, seq, D), F32), grid=(B, seq // tr),
        in_specs=[rows, rows, pl.BlockSpec((1, D), lambda b, i: (0, 0))],
        out_specs=pl.BlockSpec((None, tr, D), lambda b, i: (b, i, 0)),
        compiler_params=_params(("parallel", "parallel")), name="resid_prompt_out")(h, m, gpost)


def _kv_export_kernel(*refs, depth, H, dv):
    ko_ref, vo_ref = refs[-2:]
    for l in range(depth):
        @pl.when(pl.program_id(0) == l)
        def _(k_ref=refs[2 * l], v_ref=refs[2 * l + 1]):
            tt = k_ref.shape[0]
            for h in range(H):
                ko_ref[pl.ds(h, tt, stride=H), :] = k_ref[:, h * dv:(h + 1) * dv]
                vo_ref[pl.ds(h, tt, stride=H), :] = v_ref[:, h * dv:(h + 1) * dv]


def kv_export(zs, *, B, T, H, dv):
    depth = len(zs)
    tt = _tile(T, 344, SUBLANES)
    nt = T // tt
    last = B * nt - 1
    out = jax.ShapeDtypeStruct((depth * B * T * H, dv), F32)

    def z_spec(l, col):
        def idx(L, b, i):
            return jnp.where(L == l, b * nt + i, jnp.where(L < l, 0, last)), col
        return pl.BlockSpec((tt, H * dv), idx)

    out_spec = pl.BlockSpec((tt * H, dv), lambda L, b, i: ((L * B + b) * nt + i, 0))
    return pl.pallas_call(
        functools.partial(_kv_export_kernel, depth=depth, H=H, dv=dv), out_shape=(out, out),
        grid=(depth, B, nt),
        in_specs=[z_spec(l, col) for l in range(depth) for col in (1, 2)],
        out_specs=(out_spec, out_spec),
        compiler_params=_params(("arbitrary", "arbitrary", "arbitrary")), name="kv_export",
    )(*[z for z in zs for _ in range(2)])


def _mm_kernel(x_ref, w_ref, o_ref, *, relu2):
    acc = jnp.dot(x_ref[...], w_ref[...].astype(BF16), preferred_element_type=F32)
    if relu2:
        acc = jnp.square(jnp.maximum(acc, 0.0))
    o_ref[...] = acc.astype(o_ref.dtype)


def _x_block(tm, K, tm_target):
    mode = dict(pipeline_mode=pl.Buffered(1)) if tm_target > 1040 else {}
    return pl.BlockSpec((tm, K), lambda i, j: (i, 0), **mode)


def matmul(x, w, layer, n_out, out_dtype, relu2=False, tm_target=1040, tn_target=512):
    R, K = x.shape
    tm = _tile(R, tm_target, 16)
    tn = _tile(n_out, tn_target, LANES)
    return pl.pallas_call(
        functools.partial(_mm_kernel, relu2=relu2),
        out_shape=jax.ShapeDtypeStruct((R, n_out), out_dtype),
        grid=(R // tm, n_out // tn),
        in_specs=[_x_block(tm, K, tm_target),
                  pl.BlockSpec((None, K, tn), lambda i, j: (layer, 0, j))],
        out_specs=pl.BlockSpec((tm, tn), lambda i, j: (i, j)),
        compiler_params=_params(("parallel", "parallel")),
        name="matmul",
    )(x, w)


def _mm_nt_kernel(x_ref, w_ref, o_ref):
    w = w_ref[0].astype(BF16)
    pad = o_ref.shape[1] - w.shape[0]
    if pad:
        w = jnp.concatenate([w, jnp.zeros((pad, w.shape[1]), BF16)], axis=0)
    o_ref[...] = lax.dot_general(x_ref[...], w, _NT, preferred_element_type=F32)


def matmul_nt(x, wt, layer, row0, n_rows, tm_target=1040, tn_target=512):
    R, K = x.shape
    assert row0 % SUBLANES == 0
    tm = _tile(R, tm_target, 16)
    tn = _tile(n_rows, tn_target, LANES) if n_rows >= LANES else n_rows
    n_out = max(n_rows, LANES)
    to = max(tn, LANES)
    return pl.pallas_call(
        _mm_nt_kernel,
        out_shape=jax.ShapeDtypeStruct((R, n_out), F32),
        grid=(R // tm, n_rows // tn),
        in_specs=[_x_block(tm, K, tm_target),
                  pl.BlockSpec((pl.Element(1), pl.Element(tn), pl.Element(K)),
                               lambda i, j: (layer, pl.multiple_of(row0 + j * tn, SUBLANES), 0))],
        out_specs=pl.BlockSpec((tm, to), lambda i, j: (i, j)),
        compiler_params=_params(("parallel", "parallel")),
        name="matmul_nt",
    )(x, wt)


def _mmk_kernel(x_ref, w_ref, o_ref):
    @pl.when(pl.program_id(2) == 0)
    def _():
        o_ref[...] = jnp.zeros_like(o_ref)

    o_ref[...] += jnp.dot(x_ref[...], w_ref[...].astype(BF16), preferred_element_type=F32)


def matmul_ktiled(x, w, layer, tm_target=2080, tn_target=1024, tk_target=1024):
    R, K = x.shape
    N = w.shape[2]
    tm = _tile(R, tm_target, 16)
    tn = _tile(N, tn_target, LANES)
    tk = _tile(K, tk_target, LANES)
    return pl.pallas_call(
        _mmk_kernel,
        out_shape=jax.ShapeDtypeStruct((R, N), F32),
        grid=(R // tm, N // tn, K // tk),
        in_specs=[pl.BlockSpec((tm, tk), lambda i, j, k: (i, k)),
                  pl.BlockSpec((None, tk, tn), lambda i, j, k: (layer, k, j))],
        out_specs=pl.BlockSpec((tm, tn), lambda i, j, k: (i, j)),
        compiler_params=_params(("parallel", "parallel", "arbitrary")),
        name="matmul_ktiled",
    )(x, w)


def _lam_value(lamp_ref, lam_init):
    lp = lamp_ref[...]
    a = jnp.exp(jnp.sum(lp[0:1] * lp[1:2], axis=-1, keepdims=True))
    b = jnp.exp(jnp.sum(lp[2:3] * lp[3:4], axis=-1, keepdims=True))
    return a - b + lam_init


def _attn_prompt_kernel(slopes_ref, lamp_ref, subln_ref, subln_col_ref, q_ref, k_ref, v_ref, _mix_ref, o_ref,
                        kb_ref, vt_ref, s_ref, *, nm, qb, nblk, dh, lam_init):
    T = q_ref.shape[0]
    dv = 2 * dh
    pad = LANES - nm
    log2e = math.log2(math.e)
    slope = slopes_ref[pl.program_id(1)] * log2e
    lam = _lam_value(lamp_ref, lam_init)
    scale = dh ** -0.5 * log2e

    lane = lax.broadcasted_iota(jnp.int32, (1, dv), 1)
    first_map = lane < dh

    kb_ref[0:pad, 0:dv] = jnp.zeros((pad, dv), BF16)
    kb_ref[pad:pad + T, 0:dv] = k_ref[...].astype(BF16)
    kbias = slope * (lax.broadcasted_iota(jnp.int32, (pad + T, dv), 0) - pad).astype(F32)
    hi = kbias.astype(BF16).astype(F32)
    mid = (kbias - hi).astype(BF16).astype(F32)
    lo = kbias - hi - mid
    kb_ref[:, dv:2 * dv] = jnp.where(lane == 0, hi, jnp.where(lane == 1, mid, jnp.where(lane == 2, lo, 0.0))
                                     ).astype(BF16)
    vt_ref[...] = jnp.concatenate([jnp.zeros((pad, dv), F32), v_ref[...]], axis=0).T.astype(BF16)
    ones3 = jnp.where(lane < 3, 1.0, 0.0)

    def stacked_q(q):
        qs = q * scale
        qq = jnp.concatenate([jnp.where(first_map, qs, 0.0), jnp.where(first_map, 0.0, qs)], axis=0)
        return jnp.concatenate([qq, jnp.broadcast_to(ones3, qq.shape)], axis=1).astype(BF16)

    qq = stacked_q(q_ref[0:nm, :])
    s = lax.dot_general(qq, kb_ref[0:LANES, :], _NT, preferred_element_type=F32)
    col = lax.broadcasted_iota(jnp.int32, (2 * nm, LANES), 1)
    row = lax.broadcasted_iota(jnp.int32, (2 * nm, LANES), 0)
    row = jnp.where(row >= nm, row - nm, row)
    kpos = col - pad
    s = jnp.where((kpos >= 0) & (kpos <= row), s, NEG)
    m = jnp.max(s, axis=-1, keepdims=True)
    p = jnp.exp2(s - m)
    l = jnp.sum(p, axis=-1, keepdims=True)
    o = lax.dot_general(p.astype(BF16), vt_ref[:, 0:LANES], _NT, preferred_element_type=F32) / l
    res = o[:nm] - lam * o[nm:]
    o_ref[0:nm, :] = (_rms(res, subln_ref[...]) * (1.0 - lam_init)).astype(o_ref.dtype)

    key_ok = lax.broadcasted_iota(jnp.int32, (LANES, 2 * qb), 0) >= pad
    ql = lax.broadcasted_iota(jnp.int32, (qb, 2 * qb), 1)
    ql = jnp.where(ql >= qb, ql - qb, ql)
    causal = lax.broadcasted_iota(jnp.int32, (qb, 2 * qb), 0) <= ql

    for qj in range(nblk):
        r0 = nm + qj * qb
        nk = LANES + (qj + 1) * qb
        qq = stacked_q(q_ref[r0:r0 + qb, :])
        s = lax.dot_general(kb_ref[0:nk, :], qq, _NT, preferred_element_type=F32)
        parts = [jnp.where(key_ok, s[0:LANES, :], NEG)]
        if nk - qb > LANES:
            parts.append(s[LANES:nk - qb, :])
        parts.append(jnp.where(causal, s[nk - qb:nk, :], NEG))
        s_ref[0:nk, :] = jnp.concatenate(parts, axis=0)
        m = jnp.max(s_ref[0:nk, :], axis=0, keepdims=True)
        p = jnp.exp2(s_ref[0:nk, :] - m)
        l = jnp.sum(p, axis=0, keepdims=True)
        o = jnp.dot(vt_ref[:, 0:nk], p.astype(BF16), preferred_element_type=F32) / l
        res = o[:, :qb] - lam * o[:, qb:]
        y = res * lax.rsqrt(jnp.mean(res * res, axis=0, keepdims=True) + EPS) * subln_col_ref[...]
        o_ref[r0:r0 + qb, :] = (y * (1.0 - lam_init)).T.astype(o_ref.dtype)


def attn_prompt(z, slopes, lamp, subln, mix, *, B, T, nm, H, dh, lam_init):
    seq = T - nm
    qb = _tile(seq, 256, LANES)
    nblk = seq // qb
    dv = 2 * dh
    kern = functools.partial(_attn_prompt_kernel, nm=nm, qb=qb, nblk=nblk, dh=dh, lam_init=lam_init)
    return pl.pallas_call(
        kern,
        out_shape=jax.ShapeDtypeStruct(mix.shape, mix.dtype),
        grid=(B, H),
        in_specs=[pl.BlockSpec(memory_space=pltpu.SMEM),
                  pl.BlockSpec((4, dh), lambda b, h: (0, 0)),
                  pl.BlockSpec((1, dv), lambda b, h: (0, 0)),
                  pl.BlockSpec((dv, 1), lambda b, h: (0, 0)),
                  pl.BlockSpec((T, dv), lambda b, h: (b, h)),
                  pl.BlockSpec((T, dv), lambda b, h: (b, H + h)),
                  pl.BlockSpec((T, dv), lambda b, h: (b, 2 * H + h)),
                  pl.BlockSpec(memory_space=pl.ANY)],
        out_specs=pl.BlockSpec((T, dv), lambda b, h: (b, h)),
        scratch_shapes=[pltpu.VMEM((LANES + seq, 2 * dv), BF16), pltpu.VMEM((dv, LANES + seq), BF16),
                        pltpu.VMEM((LANES + seq, 2 * qb), F32)],
        input_output_aliases={7: 0},
        compiler_params=_params(("parallel", "parallel")),
        name="attn_prompt",
    )(slopes, lamp, subln, subln.reshape(dv, 1), z, z, z, mix)


def _attn_sample_kernel(pt_ref, lamp_ref, subln_ref, slope_ref, q_ref, kn_ref, vn_ref, *rest,
                        G, page, past_len, H, S, dv, lam_init):
    k_refs, v_refs = rest[:G], rest[G:2 * G]
    o_ref, m_ref, l_ref, acc_ref, bias_ref = rest[2 * G:]
    j = pl.program_id(1)
    nrow = H * 2 * S
    h_shift, s_shift = H.bit_length() - 1, (2 * S).bit_length() - 1
    row = lax.broadcasted_iota(jnp.int32, (nrow, 1), 0)
    row_head = lax.shift_right_logical(row, s_shift)
    slope = slope_ref[...]

    def block_bias(n, limit=None):
        col = lax.broadcasted_iota(jnp.int32, (1, n), 1)
        key = lax.shift_right_logical(col, h_shift)
        ok = (col & (H - 1)) == row_head
        if limit is not None:
            ok = ok & (key <= limit)
        return jnp.where(ok, slope * key.astype(F32), NEG)

    @pl.when(j == 0)
    def _():
        m_ref[...] = jnp.full(m_ref.shape, NEG, F32)
        l_ref[...] = jnp.zeros(l_ref.shape, F32)
        acc_ref[...] = jnp.zeros(acc_ref.shape, F32)
        bias_ref[...] = block_bias(bias_ref.shape[1])

    def update(k_parts, v_parts, bias, shift):
        q = q_ref[...]
        s = jnp.concatenate([lax.dot_general(q, r[...].astype(BF16), _NT, preferred_element_type=F32)
                             for r in k_parts], axis=1) + bias
        m_old = m_ref[...]
        m_new = jnp.maximum(m_old, jnp.max(s, axis=-1, keepdims=True) + shift)
        alpha = jnp.exp2(m_old - m_new)
        p = jnp.exp2(s - (m_new - shift))
        l_ref[...] = alpha * l_ref[...] + jnp.sum(p, axis=-1, keepdims=True)
        p = p.astype(BF16)
        pv, c0 = None, 0
        for r in v_parts:
            d = jnp.dot(p[:, c0:c0 + r.shape[0]], r[...].astype(BF16), preferred_element_type=F32)
            pv = d if pv is None else pv + d
            c0 += r.shape[0]
        acc_ref[...] = alpha * acc_ref[...] + pv
        m_ref[...] = m_new

    update(k_refs, v_refs, bias_ref[...], slope * (j * (G * page) - past_len).astype(F32))

    @pl.when(j == pl.num_programs(1) - 1)
    def _():
        update([kn_ref], [vn_ref], block_bias(kn_ref.shape[0], row & (S - 1)), 0.0)
        lam = _lam_value(lamp_ref, lam_init)
        o = acc_ref[...] / l_ref[...]
        for h in range(H):
            blk = o[h * 2 * S:(h + 1) * 2 * S, :]
            res = blk[:S] - lam * blk[S:]
            o_ref[:, h * dv:(h + 1) * dv] = _rms(res, subln_ref[...]) * (1.0 - lam_init)


def attn_sample(qrows, knew, vnew, slope_col, lamp, subln, cache_k, cache_v, page_table, layer, *,
                DB, S, H, dh, lam_init):
    dv = 2 * dh
    n_pages = page_table.shape[1]
    page = cache_k.shape[2]
    assert H & (H - 1) == 0 and S & (S - 1) == 0
    G = _tile(n_pages, 8, 1)
    ck = cache_k.reshape(cache_k.shape[0], cache_k.shape[1], page * H, dv)
    cv = cache_v.reshape(cache_v.shape[0], cache_v.shape[1], page * H, dv)
    nrow = H * 2 * S

    def page_spec(g):
        return pl.BlockSpec((None, None, page * H, dv), lambda b, j, pt: (layer, pt[b, j * G + g], 0, 0))

    kern = functools.partial(_attn_sample_kernel, G=G, page=page, past_len=n_pages * page,
                             H=H, S=S, dv=dv, lam_init=lam_init)
    per_b = lambda rows: pl.BlockSpec((None, rows, dv), lambda b, j, pt: (b, 0, 0))
    grid_spec = pltpu.PrefetchScalarGridSpec(
        num_scalar_prefetch=1,
        grid=(DB, n_pages // G),
        in_specs=[pl.BlockSpec((4, dh), lambda b, j, pt: (0, 0)),
                  pl.BlockSpec((1, dv), lambda b, j, pt: (0, 0)),
                  pl.BlockSpec((nrow, 1), lambda b, j, pt: (0, 0)),
                  per_b(nrow), per_b(knew.shape[1]), per_b(vnew.shape[1])]
                 + [page_spec(g) for g in range(G)] + [page_spec(g) for g in range(G)],
        out_specs=pl.BlockSpec((S, H * dv), lambda b, j, pt: (b, 0)),
        scratch_shapes=[pltpu.VMEM((nrow, 1), F32), pltpu.VMEM((nrow, 1), F32),
                        pltpu.VMEM((nrow, dv), F32), pltpu.VMEM((nrow, G * page * H), F32)],
    )
    return pl.pallas_call(
        kern, out_shape=jax.ShapeDtypeStruct((DB * S, H * dv), F32), grid_spec=grid_spec,
        compiler_params=_params(("parallel", "arbitrary")), name="attn_sample",
    )(page_table, lamp, subln, slope_col, qrows, knew, vnew, *([ck] * G), *([cv] * G))


def _gla_kernel(q_ref, k_ref, v_ref, g_ref, r_ref, wg_ref, bg_ref, norm_ref, s0_ref, *rest,
                blocks, dk, dvb, hp):
    o_ref, sout_ref = rest[-2:]
    L = LANES
    ri = lax.broadcasted_iota(jnp.int32, (L, L), 0)
    ci = lax.broadcasted_iota(jnp.int32, (L, L), 1)
    tril = ri >= ci
    tril_bf = jnp.where(tril, 1.0, 0.0).astype(BF16)
    rowi = lax.broadcasted_iota(jnp.int32, (L, 1), 0)
    qscale = dk ** -0.5
    states = [s0_ref[hh] for hh in range(hp)]

    def load(ref, row0, nreal, off, c0, width):
        x = ref[row0:row0 + nreal, c0:c0 + width]
        if nreal == L:
            return x
        parts = []
        if off:
            parts.append(jnp.zeros((off, width), x.dtype))
        parts.append(x)
        if L - off - nreal:
            parts.append(jnp.zeros((L - off - nreal, width), x.dtype))
        return jnp.concatenate(parts, axis=0)

    r_all = jnp.concatenate([load(r_ref, *blk, 0, L) for blk in blocks], axis=0).astype(BF16)
    x_all = jnp.dot(r_all, wg_ref[...].astype(BF16), preferred_element_type=F32) + bg_ref[...]
    gk_all = (jnp.minimum(x_all, 0.0) - jnp.log1p(jnp.exp(-jnp.abs(x_all)))) * (1.0 / GLA_TAU)
    pieces = []
    for n, (row0, nreal, off) in enumerate(blocks):
        g_n = gk_all[n * L:(n + 1) * L, :]
        if nreal < L:
            g_n = jnp.where((rowi >= off) & (rowi < off + nreal), g_n, 0.0)
        pieces.append(g_n)
    g_wide = jnp.concatenate(pieces, axis=1)
    hi = g_wide.astype(BF16)
    lo = (g_wide - hi.astype(F32)).astype(BF16)
    b_wide = (jnp.dot(tril_bf, hi, preferred_element_type=F32)
              + jnp.dot(tril_bf, lo, preferred_element_type=F32))

    for n, (row0, nreal, off) in enumerate(blocks):
        for hh in range(hp):
            S = states[hh]
            q = load(q_ref, row0, nreal, off, hh * dk, dk) * qscale
            k = load(k_ref, row0, nreal, off, hh * dk, dk)
            v = load(v_ref, row0, nreal, off, hh * dvb, dvb).astype(BF16)
            b = b_wide[:, (n * hp + hh) * dk:(n * hp + hh + 1) * dk]
            b_last = b[L - 1:L, :]
            b_mid = b[L // 2 - 1:L // 2, :]
            qs = (q * jnp.exp(b)).astype(BF16)
            qm = (q * jnp.exp(b - b_mid)).astype(BF16)
            km = (k * jnp.exp(b_mid - b)).astype(BF16)
            att = lax.dot_general(qm, km, _NT, preferred_element_type=F32)
            att = jnp.where(tril, att, 0.0).astype(BF16)
            o = jnp.dot(jnp.concatenate([att, qs], axis=1),
                        jnp.concatenate([v, S.astype(BF16)], axis=0), preferred_element_type=F32)
            kk = (k * jnp.exp(b_last - b)).T.astype(BF16)
            decay = jnp.broadcast_to(jnp.exp(b_last), (L, dk)).T
            states[hh] = (jnp.concatenate([decay] * (dvb // L), axis=1) * S
                          + jnp.dot(kk, v, preferred_element_type=F32))
            y = _rms(o, norm_ref[...])
            g = load(g_ref, row0, nreal, off, hh * dvb, dvb)
            y = y * (g / (1.0 + jnp.exp(-g)))
            o_ref[row0:row0 + nreal, hh * dvb:(hh + 1) * dvb] = y[off:off + nreal, :].astype(o_ref.dtype)
    for hh in range(hp):
        sout_ref[hh] = states[hh]


def gla(z, zr, wg, bgate, norm, s0, layer, *, row0, NB, T, H, dk, dvb, cols, blocks, mix=None, mix_col0=0):
    hp = 2 if H % 2 == 0 else 1
    cq, ck_, cv_, cg_ = cols
    wq, wv = hp * dk, hp * dvb
    assert cq % wq == 0 and ck_ % wq == 0 and cv_ % wv == 0 and cg_ % wv == 0 and mix_col0 % wv == 0
    rb = row0 // T
    kern = functools.partial(_gla_kernel, blocks=blocks, dk=dk, dvb=dvb, hp=hp)
    in_specs = [pl.BlockSpec((T, wq), lambda b, g: (rb + b, cq // wq + g)),
                pl.BlockSpec((T, wq), lambda b, g: (rb + b, ck_ // wq + g)),
                pl.BlockSpec((T, wv), lambda b, g: (rb + b, cv_ // wv + g)),
                pl.BlockSpec((T, wv), lambda b, g: (rb + b, cg_ // wv + g)),
                pl.BlockSpec((T, LANES), lambda b, g: (rb + b, 0)),
                pl.BlockSpec((None, LANES, wq), lambda b, g: (layer, 0, g)),
                pl.BlockSpec((None, 1, wq), lambda b, g: (layer, 0, g)),
                pl.BlockSpec((None, 1, dvb), lambda b, g: (layer, 0, 0)),
                pl.BlockSpec((None, hp, dk, dvb), lambda b, g: (b, g, 0, 0))]
    args = [z, z, z, z, zr, wg, bgate, norm, s0]
    aliases = {}
    if mix is None:
        y_shape = jax.ShapeDtypeStruct((NB * T, H * dvb), F32)
    else:
        y_shape = jax.ShapeDtypeStruct(mix.shape, mix.dtype)
        in_specs.append(pl.BlockSpec(memory_space=pl.ANY))
        args.append(mix)
        aliases = {len(args) - 1: 0}
    return pl.pallas_call(
        kern,
        out_shape=(y_shape, jax.ShapeDtypeStruct((NB, H, dk, dvb), F32)),
        grid=(NB, H // hp),
        in_specs=in_specs,
        out_specs=(pl.BlockSpec((T, wv), lambda b, g: (b, mix_col0 // wv + g)),
                   pl.BlockSpec((None, hp, dk, dvb), lambda b, g: (b, g, 0, 0))),
        input_output_aliases=aliases,
        compiler_params=_params(("parallel", "parallel")),
        name="gla",
    )(*args)


def _lru_kernel(cx_ref, cg_ref, buf_ref, h0_ref, cw_ref, cb_ref, wa_ref, ba_ref, wx_ref, bx_ref, lam_ref,
                *rest, T, cw, bw, conv_w):
    y_ref, hl_ref, nb_ref, e_ref, a_ref, x_ref = rest[-6:]
    seg = T // SUBLANES
    e_ref[0:SUBLANES, :] = buf_ref[...]
    e_ref[SUBLANES:SUBLANES + T, :] = cx_ref[...]
    first = SUBLANES - (conv_w - 1)
    u = cb_ref[...] + cw_ref[0:1, :] * e_ref[first:first + T, :]
    for j in range(1, conv_w):
        u = u + cw_ref[j:j + 1, :] * e_ref[first + j:first + j + T, :]
    nb_ref[...] = e_ref[T:T + SUBLANES, :]

    lam = lam_ref[...]
    sp = jnp.maximum(-lam, 0.0) + jnp.log1p(jnp.exp(-jnp.abs(lam)))
    for n in range(cw // bw):
        sl = slice(n * bw, (n + 1) * bw)
        un = u[:, sl]
        ub = un.astype(BF16)
        r = jax.nn.sigmoid(jnp.dot(ub, wa_ref[n].astype(BF16), preferred_element_type=F32) + ba_ref[:, sl])
        i = jax.nn.sigmoid(jnp.dot(ub, wx_ref[n].astype(BF16), preferred_element_type=F32) + bx_ref[:, sl])
        log_a = (-LRU_C) * r * sp[:, sl]
        a = jnp.exp(log_a)
        a_ref[n] = a
        y = (1.0 - a) * (1.0 + a)
        x_ref[n] = jnp.where(y > 0.0, y * lax.rsqrt(y), 0.0) * (i * un)

    nl = cw // bw
    rows = lambda t: pl.ds(t, SUBLANES, stride=seg)
    sub = lax.broadcasted_iota(jnp.int32, (SUBLANES, bw), 0)
    bcast = lambda v, s: jnp.broadcast_to(v[s:s + 1, :], (SUBLANES, bw))

    def pass1(t, carry):
        hs, ps = carry
        new_h, new_p = [], []
        for n in range(nl):
            a8 = a_ref[n, rows(t), :]
            h = a8 * hs[n] + x_ref[n, rows(t), :]
            p = a8 * ps[n]
            x_ref[n, rows(t), :] = h
            a_ref[n, rows(t), :] = p
            new_h.append(h)
            new_p.append(p)
        return tuple(new_h), tuple(new_p)

    h_init = tuple(jnp.where(sub == 0, jnp.broadcast_to(h0_ref[:, n * bw:(n + 1) * bw], (SUBLANES, bw)), 0.0)
                   for n in range(nl))
    p_init = tuple(jnp.ones((SUBLANES, bw), F32) for _ in range(nl))
    h_end, p_end = lax.fori_loop(0, seg, pass1, (h_init, p_init))

    carry_in = []
    for n in range(nl):
        e = bcast(h_end[n], 0)
        c = jnp.zeros((SUBLANES, bw), F32)
        for s in range(1, SUBLANES):
            c = jnp.where(sub == s, e, c)
            e = bcast(h_end[n], s) + bcast(p_end[n], s) * e
        carry_in.append(c)
        hl_ref[:, n * bw:(n + 1) * bw] = e[0:1, :]

    def pass2(t, _):
        for n in range(nl):
            x_ref[n, rows(t), :] = x_ref[n, rows(t), :] + a_ref[n, rows(t), :] * carry_in[n]
        return 0

    lax.fori_loop(0, seg, pass2, 0)
    for n in range(nl):
        sl = slice(n * bw, (n + 1) * bw)
        y_ref[:, sl] = (x_ref[n] * jax.nn.gelu(cg_ref[:, sl], approximate=True)).astype(y_ref.dtype)


def rglru(zc, bufp, h0, conv_w, conv_b, w_a, b_a, w_x, b_x, lam, layer, *, row0, NB, T, d_lru,
          mix=None, mix_col0=0):
    bw = w_a.shape[2]
    assert bw == LANES and T % SUBLANES == 0
    cw = _tile(d_lru, 512, bw)
    ncg = d_lru // cw
    nbk = cw // bw
    cwid = conv_w.shape[1]
    rb = row0 // T
    assert mix_col0 % cw == 0
    kern = functools.partial(_lru_kernel, T=T, cw=cw, bw=bw, conv_w=cwid)
    vec = lambda rows: pl.BlockSpec((None, rows, cw), lambda b, j: (layer, 0, j))
    in_specs = [pl.BlockSpec((T, cw), lambda b, j: (rb + b, j)),
                pl.BlockSpec((T, cw), lambda b, j: (rb + b, ncg + j)),
                pl.BlockSpec((None, SUBLANES, cw), lambda b, j: (b, 0, j)),
                pl.BlockSpec((None, 1, cw), lambda b, j: (b, 0, j)),
                vec(cwid), vec(1),
                pl.BlockSpec((None, nbk, bw, bw), lambda b, j: (layer, j, 0, 0)), vec(1),
                pl.BlockSpec((None, nbk, bw, bw), lambda b, j: (layer, j, 0, 0)), vec(1), vec(1)]
    args = [zc, zc, bufp, h0, conv_w, conv_b, w_a, b_a, w_x, b_x, lam]
    aliases = {}
    if mix is None:
        y_shape = jax.ShapeDtypeStruct((NB * T, d_lru), F32)
    else:
        y_shape = jax.ShapeDtypeStruct(mix.shape, mix.dtype)
        in_specs.append(pl.BlockSpec(memory_space=pl.ANY))
        args.append(mix)
        aliases = {len(args) - 1: 0}
    return pl.pallas_call(
        kern,
        out_shape=(y_shape,
                   jax.ShapeDtypeStruct((NB, 1, d_lru), F32),
                   jax.ShapeDtypeStruct((NB, SUBLANES, d_lru), F32)),
        grid=(NB, ncg),
        in_specs=in_specs,
        out_specs=(pl.BlockSpec((T, cw), lambda b, j: (b, mix_col0 // cw + j)),
                   pl.BlockSpec((None, 1, cw), lambda b, j: (b, 0, j)),
                   pl.BlockSpec((None, SUBLANES, cw), lambda b, j: (b, 0, j))),
        scratch_shapes=[pltpu.VMEM((T + SUBLANES, cw), F32), pltpu.VMEM((nbk, T, bw), F32),
                        pltpu.VMEM((nbk, T, bw), F32)],
        input_output_aliases=aliases,
        compiler_params=_params(("parallel", "parallel")),
        name="rglru",
    )(*args)


def kernel(x_prompt, x_sample, cache_k, cache_v, state_gla, state_lru_h, state_lru_conv, page_table, meta_tokens, norm_mix_pre, norm_mix_post, norm_mlp_pre, norm_mlp_post, w_in, w_out, lam_q1, lam_k1, lam_q2, lam_k2, attn_subln, gla_w_gate, gla_b_gate, gla_norm, lru_conv_w, lru_conv_b, lru_w_a, lru_b_a, lru_w_x, lru_b_x, lru_lambda, mlp_w_up, mlp_w_down):
    B, seq, D = x_prompt.shape
    DB, S, _ = x_sample.shape
    nm = meta_tokens.shape[0]
    T = nm + seq
    depth = w_in.shape[0]
    H_A, dv_a = cache_k.shape[3], cache_v.shape[4]
    dh_a = dv_a // 2
    d_a = H_A * dv_a
    H_B, dk_b, dv_b = state_gla.shape[2:]
    d_b = H_B * dv_b
    rank = gla_w_gate.shape[1]
    d_lru = state_lru_h.shape[-1]
    conv_w = lru_conv_w.shape[1]
    d_ff = mlp_w_up.shape[2]
    R_P, R_S = B * T, DB * S
    assert nm <= LANES and nm % 16 == 0 and seq % LANES == 0 and S <= SUBLANES and R_P % S == 0
    assert dv_a == LANES and dk_b == LANES and dv_b % LANES == 0 and rank <= LANES

    c_ak, c_av = d_a, 2 * d_a
    c_bq = 3 * d_a
    c_bk = c_bq + H_B * dk_b
    c_bv = c_bk + H_B * dk_b
    c_bg = c_bv + d_b
    c_br = c_bg + d_b
    c_cx = c_br + rank
    n_ab = c_br
    gla_cols = (c_bq, c_bk, c_bv, c_bg)
    d_mix = d_a + d_b + d_lru

    w_in_t = jnp.swapaxes(w_in, 1, 2)
    wg = jnp.pad(gla_w_gate, ((0, 0), (0, LANES - rank), (0, 0)))
    row3 = lambda a: a.reshape(depth, 1, a.shape[-1])

    slopes = jnp.exp2(-8.0 * jnp.arange(1, H_A + 1, dtype=F32) / H_A)
    log2e = math.log2(math.e)
    slope_col = jnp.repeat(slopes * log2e, 2 * S).reshape(H_A * 2 * S, 1)
    eye_c = jnp.eye(2, dtype=F32)
    assert S * H_A <= LANES

    blocks_p = ((0, nm, LANES - nm),) + tuple((nm + LANES * n, LANES, 0) for n in range(seq // LANES))
    blocks_s = ((0, S, 0),)

    h = jnp.concatenate([piece for b in range(B) for piece in (meta_tokens, x_prompt[b])]
                        + [x_sample.reshape(R_S, D)], axis=0)
    zeros_gla = jnp.zeros((B, H_B, dk_b, dv_b), F32)
    zeros_h = jnp.zeros((B, 1, d_lru), F32)
    zeros_buf = jnp.zeros((B, SUBLANES, d_lru), F32)

    outs = [[] for _ in range(8)]
    zs = []
    hn = rms_cast(h, norm_mix_pre[0:1])
    for l in range(depth):
        lam_init = 0.8 - 0.6 * math.exp(-0.3 * l)
        lamp = jnp.stack([lam_q1[l], lam_k1[l], lam_q2[l], lam_k2[l]])
        subln = attn_subln[l:l + 1]

        tm = 2080
        z = matmul_nt(hn, w_in_t, l, 0, n_ab, tm_target=tm)
        zc = matmul_nt(hn, w_in_t, l, c_cx, 2 * d_lru, tm_target=tm)
        zr = matmul_nt(hn, w_in_t, l, c_br, rank)
        zs.append(z)

        gla_args = (wg, row3(gla_b_gate), row3(gla_norm))
        lru_args = (lru_conv_w, row3(lru_conv_b), lru_w_a, row3(lru_b_a), lru_w_x, row3(lru_b_x),
                    row3(lru_lambda))

        q_s = z[R_P:, :d_a].reshape(DB, S, H_A, 2, dh_a) * (dh_a ** -0.5 * log2e)
        qrows = jnp.einsum('bihcd,cy->bhciyd', q_s, eye_c).reshape(DB, H_A * 2 * S, dv_a).astype(BF16)
        new_rows = lambda c0: jnp.pad(z[R_P:, c0:c0 + d_a].reshape(DB, S * H_A, dv_a),
                                      ((0, 0), (0, LANES - S * H_A), (0, 0)))
        oa_s = attn_sample(qrows, new_rows(c_ak), new_rows(c_av), slope_col, lamp, subln, cache_k, cache_v,
                           page_table, l, DB=DB, S=S, H=H_A, dh=dh_a, lam_init=lam_init)
        ob_s, sg_s = gla(z, zr, *gla_args, state_gla[l], l, row0=R_P, NB=DB, T=S, H=H_B, dk=dk_b, dvb=dv_b,
                         cols=gla_cols, blocks=blocks_s)
        bufp = jnp.pad(state_lru_conv[l], ((0, 0), (SUBLANES - (conv_w - 1), 0), (0, 0)))
        oc_s, hl_s, nb_s = rglru(zc, bufp, state_lru_h[l][:, None, :], *lru_args, l, row0=R_P, NB=DB, T=S,
                                 d_lru=d_lru)

        mix = jnp.pad(jnp.concatenate([oa_s, ob_s, oc_s], axis=1).astype(BF16), ((R_P, 0), (0, 0)))
        mix = attn_prompt(z, slopes, lamp, subln, mix, B=B, T=T, nm=nm, H=H_A, dh=dh_a, lam_init=lam_init)
        mix, sg_p = gla(z, zr, *gla_args, zeros_gla, l, row0=0, NB=B, T=T, H=H_B, dk=dk_b, dvb=dv_b,
                        cols=gla_cols, blocks=blocks_p, mix=mix, mix_col0=d_a)
        mix, hl_p, nb_p = rglru(zc, zeros_buf, zeros_h, *lru_args, l, row0=0, NB=B, T=T, d_lru=d_lru,
                                mix=mix, mix_col0=d_a + d_b)
        m1 = matmul(mix, w_out, l, D, F32, tm_target=tm)
        h, hn2 = resid_norm(h, m1, norm_mix_post[l:l + 1], norm_mlp_pre[l:l + 1])
        up = matmul(hn2, mlp_w_up, l, d_ff, BF16, relu2=True, tm_target=tm)
        dn = matmul_ktiled(up, mlp_w_down, l)
        if l + 1 < depth:
            h, hn = resid_norm(h, dn, norm_mlp_post[l:l + 1], norm_mix_pre[l + 1:l + 2])

        new = (z[R_P:, c_ak:c_av].reshape(DB, S, H_A, dv_a), z[R_P:, c_av:c_bq].reshape(DB, S, H_A, dv_a),
               sg_p, sg_s, hl_p[:, 0, :], hl_s[:, 0, :],
               nb_p[:, SUBLANES - (conv_w - 1):, :], nb_s[:, SUBLANES - (conv_w - 1):, :])
        for acc, val in zip(outs, new):
            acc.append(val)

    gpost = norm_mlp_post[depth - 1:depth]
    y_prompt = resid_prompt_out(h, dn, gpost, B=B, T=T, nm=nm)
    y_sample = resid_norm(h[R_P:], dn[R_P:], gpost)[0].reshape(DB, S, D)
    k_prompt, v_prompt = (a.reshape(depth, B, T, H_A, dv_a) for a in kv_export(zs, B=B, T=T, H=H_A, dv=dv_a))
    return (y_prompt, y_sample, k_prompt, v_prompt) + tuple(jnp.stack(o) for o in outs)
```

```python
import functools
import math

import jax
import jax.numpy as jnp
from jax import lax
from jax.experimental import pallas as pl
from jax.experimental.pallas import tpu as pltpu

F32 = jnp.float32
BF16 = jnp.bfloat16
EPS = 1e-6
GLA_TAU = 16.0
LRU_C = 8.0
NEG = -1e30
LANES = 128
SUBLANES = 8
VMEM_LIMIT_BYTES = 56 * 1024 * 1024

_NT = (((1,), (1,)), ((), ()))


def _tile(n, target, mult):
    best = None
    for d in range(mult, min(n, target) + 1, mult):
        if n % d == 0:
            best = d
    return n if best is None else best


def _params(sem):
    return pltpu.CompilerParams(dimension_semantics=sem, vmem_limit_bytes=VMEM_LIMIT_BYTES)


def _rms(x, g):
    return x * lax.rsqrt(jnp.mean(x * x, axis=-1, keepdims=True) + EPS) * g


def _rms_cast_kernel(x_ref, g_ref, o_ref):
    o_ref[...] = _rms(x_ref[...], g_ref[...]).astype(o_ref.dtype)


def rms_cast(x, g):
    R, D = x.shape
    tr = _tile(R, 416, 16)
    return pl.pallas_call(
        _rms_cast_kernel,
        out_shape=jax.ShapeDtypeStruct((R, D), BF16),
        grid=(R // tr,),
        in_specs=[pl.BlockSpec((tr, D), lambda i: (i, 0)), pl.BlockSpec((1, D), lambda i: (0, 0))],
        out_specs=pl.BlockSpec((tr, D), lambda i: (i, 0)),
        compiler_params=_params(("parallel",)),
        name="rms_cast",
    )(x, g)


def _resid_kernel(h_ref, m_ref, gpost_ref, o_ref):
    o_ref[...] = h_ref[...] + _rms(m_ref[...], gpost_ref[...])


def _resid_next_kernel(h_ref, m_ref, gpost_ref, gpre_ref, o_ref, hn_ref):
    h = h_ref[...] + _rms(m_ref[...], gpost_ref[...])
    o_ref[...] = h
    hn_ref[...] = _rms(h, gpre_ref[...]).astype(hn_ref.dtype)


def resid_norm(h, m, gpost, gpre=None):
    R, D = h.shape
    tr = _tile(R, 208, 16)
    row = pl.BlockSpec((tr, D), lambda i: (i, 0))
    vec = pl.BlockSpec((1, D), lambda i: (0, 0))
    if gpre is None:
        return pl.pallas_call(
            _resid_kernel, out_shape=jax.ShapeDtypeStruct((R, D), F32), grid=(R // tr,),
            in_specs=[row, row, vec], out_specs=row, compiler_params=_params(("parallel",)),
            name="resid_norm")(h, m, gpost), None
    return pl.pallas_call(
        _resid_next_kernel,
        out_shape=(jax.ShapeDtypeStruct((R, D), F32), jax.ShapeDtypeStruct((R, D), BF16)),
        grid=(R // tr,), in_specs=[row, row, vec, vec], out_specs=(row, row),
        compiler_params=_params(("parallel",)), name="resid_norm_next")(h, m, gpost, gpre)


def resid_prompt_out(h, m, gpost, *, B, T, nm):
    D = h.shape[1]
    seq = T - nm
    tr = _tile(seq, 256, SUBLANES)
    rows = pl.BlockSpec((pl.Element(tr), pl.Element(D)),
                        lambda b, i: (pl.multiple_of(b * T + nm + i * tr, SUBLANES), 0))
    return pl.pallas_call(
        _resid_kernel, out_shape=jax.ShapeDtypeStruct((B, seq, D), F32), grid=(B, seq // tr),
        in_specs=[rows, rows, pl.BlockSpec((1, D), lambda b, i: (0, 0))],
        out_specs=pl.BlockSpec((None, tr, D), lambda b, i: (b, i, 0)),
        compiler_params=_params(("parallel", "parallel")), name="resid_prompt_out")(h, m, gpost)


def _kv_export_kernel(*refs, depth, H, dv):
    ko_ref, vo_ref = refs[-2:]
    for l in range(depth):
        @pl.when(pl.program_id(0) == l)
        def _(k_ref=refs[2 * l], v_ref=refs[2 * l + 1]):
            tt = k_ref.shape[0]
            for h in range(H):
                ko_ref[pl.ds(h, tt, stride=H), :] = k_ref[:, h * dv:(h + 1) * dv]
                vo_ref[pl.ds(h, tt, stride=H), :] = v_ref[:, h * dv:(h + 1) * dv]


def kv_export(zs, *, B, T, H, dv):
    depth = len(zs)
    tt = _tile(T, 344, SUBLANES)
    nt = T // tt
    last = B * nt - 1
    out = jax.ShapeDtypeStruct((depth * B * T * H, dv), F32)

    def z_spec(l, col):
        def idx(L, b, i):
            return jnp.where(L == l, b * nt + i, jnp.where(L < l, 0, last)), col
        return pl.BlockSpec((tt, H * dv), idx)

    out_spec = pl.BlockSpec((tt * H, dv), lambda L, b, i: ((L * B + b) * nt + i, 0))
    return pl.pallas_call(
        functools.partial(_kv_export_kernel, depth=depth, H=H, dv=dv), out_shape=(out, out),
        grid=(depth, B, nt),
        in_specs=[z_spec(l, col) for l in range(depth) for col in (1, 2)],
        out_specs=(out_spec, out_spec),
        compiler_params=_params(("arbitrary", "arbitrary", "arbitrary")), name="kv_export",
    )(*[z for z in zs for _ in range(2)])


def _mm_kernel(x_ref, w_ref, o_ref, *, relu2):
    acc = jnp.dot(x_ref[...], w_ref[...].astype(BF16), preferred_element_type=F32)
    if relu2:
        acc = jnp.square(jnp.maximum(acc, 0.0))
    o_ref[...] = acc.astype(o_ref.dtype)


def _x_block(tm, K, tm_target):
    mode = dict(pipeline_mode=pl.Buffered(1)) if tm_target > 1040 else {}
    return pl.BlockSpec((tm, K), lambda i, j: (i, 0), **mode)


def matmul(x, w, layer, n_out, out_dtype, relu2=False, tm_target=1040, tn_target=512):
    R, K = x.shape
    tm = _tile(R, tm_target, 16)
    tn = _tile(n_out, tn_target, LANES)
    return pl.pallas_call(
        functools.partial(_mm_kernel, relu2=relu2),
        out_shape=jax.ShapeDtypeStruct((R, n_out), out_dtype),
        grid=(R // tm, n_out // tn),
        in_specs=[_x_block(tm, K, tm_target),
                  pl.BlockSpec((None, K, tn), lambda i, j: (layer, 0, j))],
        out_specs=pl.BlockSpec((tm, tn), lambda i, j: (i, j)),
        compiler_params=_params(("parallel", "parallel")),
        name="matmul",
    )(x, w)


def _mm_nt_kernel(x_ref, w_ref, o_ref):
    w = w_ref[0].astype(BF16)
    pad = o_ref.shape[1] - w.shape[0]
    if pad:
        w = jnp.concatenate([w, jnp.zeros((pad, w.shape[1]), BF16)], axis=0)
    o_ref[...] = lax.dot_general(x_ref[...], w, _NT, preferred_element_type=F32)


def matmul_nt(x, wt, layer, row0, n_rows, tm_target=1040, tn_target=512):
    R, K = x.shape
    assert row0 % SUBLANES == 0
    tm = _tile(R, tm_target, 16)
    tn = _tile(n_rows, tn_target, LANES) if n_rows >= LANES else n_rows
    n_out = max(n_rows, LANES)
    to = max(tn, LANES)
    return pl.pallas_call(
        _mm_nt_kernel,
        out_shape=jax.ShapeDtypeStruct((R, n_out), F32),
        grid=(R // tm, n_rows // tn),
        in_specs=[_x_block(tm, K, tm_target),
                  pl.BlockSpec((pl.Element(1), pl.Element(tn), pl.Element(K)),
                               lambda i, j: (layer, pl.multiple_of(row0 + j * tn, SUBLANES), 0))],
        out_specs=pl.BlockSpec((tm, to), lambda i, j: (i, j)),
        compiler_params=_params(("parallel", "parallel")),
        name="matmul_nt",
    )(x, wt)


def _mmk_kernel(x_ref, w_ref, o_ref):
    @pl.when(pl.program_id(2) == 0)
    def _():
        o_ref[...] = jnp.zeros_like(o_ref)

    o_ref[...] += jnp.dot(x_ref[...], w_ref[...].astype(BF16), preferred_element_type=F32)


def matmul_ktiled(x, w, layer, tm_target=2080, tn_target=1024, tk_target=1024):
    R, K = x.shape
    N = w.shape[2]
    tm = _tile(R, tm_target, 16)
    tn = _tile(N, tn_target, LANES)
    tk = _tile(K, tk_target, LANES)
    return pl.pallas_call(
        _mmk_kernel,
        out_shape=jax.ShapeDtypeStruct((R, N), F32),
        grid=(R // tm, N // tn, K // tk),
        in_specs=[pl.BlockSpec((tm, tk), lambda i, j, k: (i, k)),
                  pl.BlockSpec((None, tk, tn), lambda i, j, k: (layer, k, j))],
        out_specs=pl.BlockSpec((tm, tn), lambda i, j, k: (i, j)),
        compiler_params=_params(("parallel", "parallel", "arbitrary")),
        name="matmul_ktiled",
    )(x, w)


def _lam_value(lamp_ref, lam_init):
    lp = lamp_ref[...]
    a = jnp.exp(jnp.sum(lp[0:1] * lp[1:2], axis=-1, keepdims=True))
    b = jnp.exp(jnp.sum(lp[2:3] * lp[3:4], axis=-1, keepdims=True))
    return a - b + lam_init


def _attn_prompt_kernel(slopes_ref, lamp_ref, subln_ref, subln_col_ref, q_ref, k_ref, v_ref, o_ref,
                        kb_ref, vt_ref, s_ref, *, nm, qb, nblk, dh, lam_init):
    T = q_ref.shape[0]
    dv = 2 * dh
    pad = LANES - nm
    log2e = math.log2(math.e)
    slope = slopes_ref[pl.program_id(1)] * log2e
    lam = _lam_value(lamp_ref, lam_init)
    scale = dh ** -0.5 * log2e

    lane = lax.broadcasted_iota(jnp.int32, (1, dv), 1)
    first_map = lane < dh

    kb_ref[0:pad, 0:dv] = jnp.zeros((pad, dv), BF16)
    kb_ref[pad:pad + T, 0:dv] = k_ref[...].astype(BF16)
    kbias = slope * (lax.broadcasted_iota(jnp.int32, (pad + T, dv), 0) - pad).astype(F32)
    hi = kbias.astype(BF16).astype(F32)
    mid = (kbias - hi).astype(BF16).astype(F32)
    lo = kbias - hi - mid
    kb_ref[:, dv:2 * dv] = jnp.where(lane == 0, hi, jnp.where(lane == 1, mid, jnp.where(lane == 2, lo, 0.0))
                                     ).astype(BF16)
    vt_ref[...] = jnp.concatenate([jnp.zeros((pad, dv), F32), v_ref[...]], axis=0).T.astype(BF16)
    ones3 = jnp.where(lane < 3, 1.0, 0.0)

    def stacked_q(q):
        qs = q * scale
        qq = jnp.concatenate([jnp.where(first_map, qs, 0.0), jnp.where(first_map, 0.0, qs)], axis=0)
        return jnp.concatenate([qq, jnp.broadcast_to(ones3, qq.shape)], axis=1).astype(BF16)

    qq = stacked_q(q_ref[0:nm, :])
    s = lax.dot_general(qq, kb_ref[0:LANES, :], _NT, preferred_element_type=F32)
    col = lax.broadcasted_iota(jnp.int32, (2 * nm, LANES), 1)
    row = lax.broadcasted_iota(jnp.int32, (2 * nm, LANES), 0)
    row = jnp.where(row >= nm, row - nm, row)
    kpos = col - pad
    s = jnp.where((kpos >= 0) & (kpos <= row), s, NEG)
    m = jnp.max(s, axis=-1, keepdims=True)
    p = jnp.exp2(s - m)
    l = jnp.sum(p, axis=-1, keepdims=True)
    o = lax.dot_general(p.astype(BF16), vt_ref[:, 0:LANES], _NT, preferred_element_type=F32) / l
    res = o[:nm] - lam * o[nm:]
    o_ref[0:nm, :] = (_rms(res, subln_ref[...]) * (1.0 - lam_init)).astype(o_ref.dtype)

    key_ok = lax.broadcasted_iota(jnp.int32, (LANES, 2 * qb), 0) >= pad
    ql = lax.broadcasted_iota(jnp.int32, (qb, 2 * qb), 1)
    ql = jnp.where(ql >= qb, ql - qb, ql)
    causal = lax.broadcasted_iota(jnp.int32, (qb, 2 * qb), 0) <= ql

    for qj in range(nblk):
        r0 = nm + qj * qb
        nk = LANES + (qj + 1) * qb
        qq = stacked_q(q_ref[r0:r0 + qb, :])
        s = lax.dot_general(kb_ref[0:nk, :], qq, _NT, preferred_element_type=F32)
        parts = [jnp.where(key_ok, s[0:LANES, :], NEG)]
        if nk - qb > LANES:
            parts.append(s[LANES:nk - qb, :])
        parts.append(jnp.where(causal, s[nk - qb:nk, :], NEG))
        s_ref[0:nk, :] = jnp.concatenate(parts, axis=0)
        m = jnp.max(s_ref[0:nk, :], axis=0, keepdims=True)
        p = jnp.exp2(s_ref[0:nk, :] - m)
        l = jnp.sum(p, axis=0, keepdims=True)
        o = jnp.dot(vt_ref[:, 0:nk], p.astype(BF16), preferred_element_type=F32) / l
        res = o[:, :qb] - lam * o[:, qb:]
        y = res * lax.rsqrt(jnp.mean(res * res, axis=0, keepdims=True) + EPS) * subln_col_ref[...]
        o_ref[r0:r0 + qb, :] = (y * (1.0 - lam_init)).T.astype(o_ref.dtype)


def attn_prompt(z, slopes, lamp, subln, *, B, T, nm, H, dh, lam_init, mix_shape):
    seq = T - nm
    qb = _tile(seq, 512, LANES)
    nblk = seq // qb
    dv = 2 * dh
    kern = functools.partial(_attn_prompt_kernel, nm=nm, qb=qb, nblk=nblk, dh=dh, lam_init=lam_init)
    return pl.pallas_call(
        kern,
        out_shape=jax.ShapeDtypeStruct(mix_shape, BF16),
        grid=(B, H),
        in_specs=[pl.BlockSpec(memory_space=pltpu.SMEM),
                  pl.BlockSpec((4, dh), lambda b, h: (0, 0)),
                  pl.BlockSpec((1, dv), lambda b, h: (0, 0)),
                  pl.BlockSpec((dv, 1), lambda b, h: (0, 0)),
                  pl.BlockSpec((T, dv), lambda b, h: (b, h)),
                  pl.BlockSpec((T, dv), lambda b, h: (b, H + h)),
                  pl.BlockSpec((T, dv), lambda b, h: (b, 2 * H + h))],
        out_specs=pl.BlockSpec((T, dv), lambda b, h: (b, h)),
        scratch_shapes=[pltpu.VMEM((LANES + seq, 2 * dv), BF16), pltpu.VMEM((dv, LANES + seq), BF16),
                        pltpu.VMEM((LANES + seq, 2 * qb), F32)],
        compiler_params=_params(("parallel", "parallel")),
        name="attn_prompt",
    )(slopes, lamp, subln, subln.reshape(dv, 1), z, z, z)


def _attn_sample_kernel(pt_ref, lamp_ref, subln_ref, slope_ref, q_ref, kn_ref, vn_ref, *rest,
                        G, page, past_len, H, S, dv, lam_init):
    k_refs, v_refs = rest[:G], rest[G:2 * G]
    o_ref, m_ref, l_ref, acc_ref, bias_ref = rest[2 * G:]
    j = pl.program_id(1)
    nrow = H * 2 * S
    h_shift, s_shift = H.bit_length() - 1, (2 * S).bit_length() - 1
    row = lax.broadcasted_iota(jnp.int32, (nrow, 1), 0)
    row_head = lax.shift_right_logical(row, s_shift)
    slope = slope_ref[...]

    def block_bias(n, limit=None):
        col = lax.broadcasted_iota(jnp.int32, (1, n), 1)
        key = lax.shift_right_logical(col, h_shift)
        ok = (col & (H - 1)) == row_head
        if limit is not None:
            ok = ok & (key <= limit)
        return jnp.where(ok, slope * key.astype(F32), NEG)

    @pl.when(j == 0)
    def _():
        m_ref[...] = jnp.full(m_ref.shape, NEG, F32)
        l_ref[...] = jnp.zeros(l_ref.shape, F32)
        acc_ref[...] = jnp.zeros(acc_ref.shape, F32)
        bias_ref[...] = block_bias(bias_ref.shape[1])

    def update(k_parts, v_parts, bias, shift):
        q = q_ref[...]
        s = jnp.concatenate([lax.dot_general(q, r[...].astype(BF16), _NT, preferred_element_type=F32)
                             for r in k_parts], axis=1) + bias
        m_old = m_ref[...]
        m_new = jnp.maximum(m_old, jnp.max(s, axis=-1, keepdims=True) + shift)
        alpha = jnp.exp2(m_old - m_new)
        p = jnp.exp2(s - (m_new - shift))
        l_ref[...] = alpha * l_ref[...] + jnp.sum(p, axis=-1, keepdims=True)
        p = p.astype(BF16)
        pv, c0 = None, 0
        for r in v_parts:
            d = jnp.dot(p[:, c0:c0 + r.shape[0]], r[...].astype(BF16), preferred_element_type=F32)
            pv = d if pv is None else pv + d
            c0 += r.shape[0]
        acc_ref[...] = alpha * acc_ref[...] + pv
        m_ref[...] = m_new

    update(k_refs, v_refs, bias_ref[...], slope * (j * (G * page) - past_len).astype(F32))

    @pl.when(j == pl.num_programs(1) - 1)
    def _():
        update([kn_ref], [vn_ref], block_bias(kn_ref.shape[0], row & (S - 1)), 0.0)
        lam = _lam_value(lamp_ref, lam_init)
        o = acc_ref[...] / l_ref[...]
        for h in range(H):
            blk = o[h * 2 * S:(h + 1) * 2 * S, :]
            res = blk[:S] - lam * blk[S:]
            o_ref[:, h * dv:(h + 1) * dv] = _rms(res, subln_ref[...]) * (1.0 - lam_init)


def attn_sample(qrows, knew, vnew, slope_col, lamp, subln, cache_k, cache_v, page_table, layer, *,
                DB, S, H, dh, lam_init):
    dv = 2 * dh
    n_pages = page_table.shape[1]
    page = cache_k.shape[2]
    assert H & (H - 1) == 0 and S & (S - 1) == 0
    G = _tile(n_pages, 8, 1)
    ck = cache_k.reshape(cache_k.shape[0], cache_k.shape[1], page * H, dv)
    cv = cache_v.reshape(cache_v.shape[0], cache_v.shape[1], page * H, dv)
    nrow = H * 2 * S

    def page_spec(g):
        return pl.BlockSpec((None, None, page * H, dv), lambda b, j, pt: (layer, pt[b, j * G + g], 0, 0))

    kern = functools.partial(_attn_sample_kernel, G=G, page=page, past_len=n_pages * page,
                             H=H, S=S, dv=dv, lam_init=lam_init)
    per_b = lambda rows: pl.BlockSpec((None, rows, dv), lambda b, j, pt: (b, 0, 0))
    grid_spec = pltpu.PrefetchScalarGridSpec(
        num_scalar_prefetch=1,
        grid=(DB, n_pages // G),
        in_specs=[pl.BlockSpec((4, dh), lambda b, j, pt: (0, 0)),
                  pl.BlockSpec((1, dv), lambda b, j, pt: (0, 0)),
                  pl.BlockSpec((nrow, 1), lambda b, j, pt: (0, 0)),
                  per_b(nrow), per_b(knew.shape[1]), per_b(vnew.shape[1])]
                 + [page_spec(g) for g in range(G)] + [page_spec(g) for g in range(G)],
        out_specs=pl.BlockSpec((S, H * dv), lambda b, j, pt: (b, 0)),
        scratch_shapes=[pltpu.VMEM((nrow, 1), F32), pltpu.VMEM((nrow, 1), F32),
                        pltpu.VMEM((nrow, dv), F32), pltpu.VMEM((nrow, G * page * H), F32)],
    )
    return pl.pallas_call(
        kern, out_shape=jax.ShapeDtypeStruct((DB * S, H * dv), F32), grid_spec=grid_spec,
        compiler_params=_params(("parallel", "arbitrary")), name="attn_sample",
    )(page_table, lamp, subln, slope_col, qrows, knew, vnew, *([ck] * G), *([cv] * G))


def _gla_kernel(q_ref, k_ref, v_ref, g_ref, r_ref, wg_ref, bg_ref, norm_ref, s0_ref, *rest,
                blocks, dk, dvb, hp):
    o_ref, sout_ref = rest[-2:]
    L = LANES
    ri = lax.broadcasted_iota(jnp.int32, (L, L), 0)
    ci = lax.broadcasted_iota(jnp.int32, (L, L), 1)
    tril = ri >= ci
    tril_bf = jnp.where(tril, 1.0, 0.0).astype(BF16)
    rowi = lax.broadcasted_iota(jnp.int32, (L, 1), 0)
    qscale = dk ** -0.5
    states = [s0_ref[hh] for hh in range(hp)]

    def load(ref, row0, nreal, off, c0, width):
        x = ref[row0:row0 + nreal, c0:c0 + width]
        if nreal == L:
            return x
        parts = []
        if off:
            parts.append(jnp.zeros((off, width), x.dtype))
        parts.append(x)
        if L - off - nreal:
            parts.append(jnp.zeros((L - off - nreal, width), x.dtype))
        return jnp.concatenate(parts, axis=0)

    r_all = jnp.concatenate([load(r_ref, *blk, 0, L) for blk in blocks], axis=0).astype(BF16)
    x_all = jnp.dot(r_all, wg_ref[...].astype(BF16), preferred_element_type=F32) + bg_ref[...]
    gk_all = (jnp.minimum(x_all, 0.0) - jnp.log1p(jnp.exp(-jnp.abs(x_all)))) * (1.0 / GLA_TAU)
    pieces = []
    for n, (row0, nreal, off) in enumerate(blocks):
        g_n = gk_all[n * L:(n + 1) * L, :]
        if nreal < L:
            g_n = jnp.where((rowi >= off) & (rowi < off + nreal), g_n, 0.0)
        pieces.append(g_n)
    g_wide = jnp.concatenate(pieces, axis=1)
    hi = g_wide.astype(BF16)
    lo = (g_wide - hi.astype(F32)).astype(BF16)
    b_wide = (jnp.dot(tril_bf, hi, preferred_element_type=F32)
              + jnp.dot(tril_bf, lo, preferred_element_type=F32))

    for n, (row0, nreal, off) in enumerate(blocks):
        for hh in range(hp):
            S = states[hh]
            q = load(q_ref, row0, nreal, off, hh * dk, dk) * qscale
            k = load(k_ref, row0, nreal, off, hh * dk, dk)
            v = load(v_ref, row0, nreal, off, hh * dvb, dvb).astype(BF16)
            b = b_wide[:, (n * hp + hh) * dk:(n * hp + hh + 1) * dk]
            b_last = b[L - 1:L, :]
            b_mid = b[L // 2 - 1:L // 2, :]
            qs = (q * jnp.exp(b)).astype(BF16)
            qm = (q * jnp.exp(b - b_mid)).astype(BF16)
            km = (k * jnp.exp(b_mid - b)).astype(BF16)
            att = lax.dot_general(qm, km, _NT, preferred_element_type=F32)
            att = jnp.where(tril, att, 0.0).astype(BF16)
            o = jnp.dot(jnp.concatenate([att, qs], axis=1),
                        jnp.concatenate([v, S.astype(BF16)], axis=0), preferred_element_type=F32)
            kk = (k * jnp.exp(b_last - b)).T.astype(BF16)
            decay = jnp.broadcast_to(jnp.exp(b_last), (L, dk)).T
            states[hh] = (jnp.concatenate([decay] * (dvb // L), axis=1) * S
                          + jnp.dot(kk, v, preferred_element_type=F32))
            y = _rms(o, norm_ref[...])
            g = load(g_ref, row0, nreal, off, hh * dvb, dvb)
            y = y * (g / (1.0 + jnp.exp(-g)))
            o_ref[row0:row0 + nreal, hh * dvb:(hh + 1) * dvb] = y[off:off + nreal, :].astype(o_ref.dtype)
    for hh in range(hp):
        sout_ref[hh] = states[hh]


def gla(z, zr, wg, bgate, norm, s0, layer, *, row0, NB, T, H, dk, dvb, cols, blocks, mix=None, mix_col0=0):
    hp = 2 if H % 2 == 0 else 1
    cq, ck_, cv_, cg_ = cols
    wq, wv = hp * dk, hp * dvb
    assert cq % wq == 0 and ck_ % wq == 0 and cv_ % wv == 0 and cg_ % wv == 0 and mix_col0 % wv == 0
    rb = row0 // T
    kern = functools.partial(_gla_kernel, blocks=blocks, dk=dk, dvb=dvb, hp=hp)
    in_specs = [pl.BlockSpec((T, wq), lambda b, g: (rb + b, cq // wq + g)),
                pl.BlockSpec((T, wq), lambda b, g: (rb + b, ck_ // wq + g)),
                pl.BlockSpec((T, wv), lambda b, g: (rb + b, cv_ // wv + g)),
                pl.BlockSpec((T, wv), lambda b, g: (rb + b, cg_ // wv + g)),
                pl.BlockSpec((T, LANES), lambda b, g: (rb + b, 0)),
                pl.BlockSpec((None, LANES, wq), lambda b, g: (layer, 0, g)),
                pl.BlockSpec((None, 1, wq), lambda b, g: (layer, 0, g)),
                pl.BlockSpec((None, 1, dvb), lambda b, g: (layer, 0, 0)),
                pl.BlockSpec((None, hp, dk, dvb), lambda b, g: (b, g, 0, 0))]
    args = [z, z, z, z, zr, wg, bgate, norm, s0]
    aliases = {}
    if mix is None:
        y_shape = jax.ShapeDtypeStruct((NB * T, H * dvb), F32)
    else:
        y_shape = jax.ShapeDtypeStruct(mix.shape, mix.dtype)
        in_specs.append(pl.BlockSpec(memory_space=pl.ANY))
        args.append(mix)
        aliases = {len(args) - 1: 0}
    return pl.pallas_call(
        kern,
        out_shape=(y_shape, jax.ShapeDtypeStruct((NB, H, dk, dvb), F32)),
        grid=(NB, H // hp),
        in_specs=in_specs,
        out_specs=(pl.BlockSpec((T, wv), lambda b, g: (b, mix_col0 // wv + g)),
                   pl.BlockSpec((None, hp, dk, dvb), lambda b, g: (b, g, 0, 0))),
        input_output_aliases=aliases,
        compiler_params=_params(("parallel", "parallel")),
        name="gla",
    )(*args)


def _lru_kernel(cx_ref, cg_ref, buf_ref, h0_ref, cw_ref, cb_ref, wa_ref, ba_ref, wx_ref, bx_ref, lam_ref,
                *rest, T, cw, bw, conv_w):
    y_ref, hl_ref, nb_ref, e_ref, a_ref, x_ref = rest[-6:]
    seg = T // SUBLANES
    e_ref[0:SUBLANES, :] = buf_ref[...]
    e_ref[SUBLANES:SUBLANES + T, :] = cx_ref[...]
    first = SUBLANES - (conv_w - 1)
    u = cb_ref[...] + cw_ref[0:1, :] * e_ref[first:first + T, :]
    for j in range(1, conv_w):
        u = u + cw_ref[j:j + 1, :] * e_ref[first + j:first + j + T, :]
    nb_ref[...] = e_ref[T:T + SUBLANES, :]

    lam = lam_ref[...]
    decay = (-LRU_C) * (jnp.maximum(-lam, 0.0) + jnp.log1p(jnp.exp(-jnp.abs(lam))))
    for n in range(cw // bw):
        sl = slice(n * bw, (n + 1) * bw)
        un = u[:, sl]
        ub = un.astype(BF16)
        r = jax.nn.sigmoid(jnp.dot(ub, wa_ref[n].astype(BF16), preferred_element_type=F32) + ba_ref[:, sl])
        i = jax.nn.sigmoid(jnp.dot(ub, wx_ref[n].astype(BF16), preferred_element_type=F32) + bx_ref[:, sl])
        log_a = r * decay[:, sl]
        a = jnp.exp(log_a)
        a_ref[n] = a
        y = (1.0 - a) * (1.0 + a)
        x_ref[n] = jnp.where(y > 0.0, y * lax.rsqrt(y), 0.0) * (i * un)

    nl = cw // bw
    rows = lambda t: pl.ds(t, SUBLANES, stride=seg)
    sub = lax.broadcasted_iota(jnp.int32, (SUBLANES, bw), 0)
    bcast = lambda v, s: jnp.broadcast_to(v[s:s + 1, :], (SUBLANES, bw))

    def pass1(t, carry):
        hs, ps = carry
        new_h, new_p = [], []
        for n in range(nl):
            a8 = a_ref[n, rows(t), :]
            h = a8 * hs[n] + x_ref[n, rows(t), :]
            p = a8 * ps[n]
            x_ref[n, rows(t), :] = h
            a_ref[n, rows(t), :] = p
            new_h.append(h)
            new_p.append(p)
        return tuple(new_h), tuple(new_p)

    h_init = tuple(jnp.where(sub == 0, jnp.broadcast_to(h0_ref[:, n * bw:(n + 1) * bw], (SUBLANES, bw)), 0.0)
                   for n in range(nl))
    p_init = tuple(jnp.ones((SUBLANES, bw), F32) for _ in range(nl))
    h_end, p_end = lax.fori_loop(0, seg, pass1, (h_init, p_init))

    carry_in = []
    for n in range(nl):
        e = bcast(h_end[n], 0)
        c = jnp.zeros((SUBLANES, bw), F32)
        for s in range(1, SUBLANES):
            c = jnp.where(sub == s, e, c)
            e = bcast(h_end[n], s) + bcast(p_end[n], s) * e
        carry_in.append(c)
        hl_ref[:, n * bw:(n + 1) * bw] = e[0:1, :]

    def pass2(t, _):
        for n in range(nl):
            x_ref[n, rows(t), :] = x_ref[n, rows(t), :] + a_ref[n, rows(t), :] * carry_in[n]
        return 0

    lax.fori_loop(0, seg, pass2, 0)
    for n in range(nl):
        sl = slice(n * bw, (n + 1) * bw)
        g = cg_ref[:, sl]
        a1 = -2.0 * math.sqrt(2.0 / math.pi) * math.log2(math.e)
        gelu = g / (1.0 + jnp.exp2(g * (a1 + (a1 * 0.044715) * (g * g))))
        y_ref[:, sl] = (x_ref[n] * gelu).astype(y_ref.dtype)


def rglru(zc, bufp, h0, conv_w, conv_b, w_a, b_a, w_x, b_x, lam, layer, *, row0, NB, T, d_lru,
          mix=None, mix_col0=0):
    bw = w_a.shape[2]
    assert bw == LANES and T % SUBLANES == 0
    cw = _tile(d_lru, 512, bw)
    ncg = d_lru // cw
    nbk = cw // bw
    cwid = conv_w.shape[1]
    rb = row0 // T
    assert mix_col0 % cw == 0
    kern = functools.partial(_lru_kernel, T=T, cw=cw, bw=bw, conv_w=cwid)
    vec = lambda rows: pl.BlockSpec((None, rows, cw), lambda b, j: (layer, 0, j))
    in_specs = [pl.BlockSpec((T, cw), lambda b, j: (rb + b, j)),
                pl.BlockSpec((T, cw), lambda b, j: (rb + b, ncg + j)),
                pl.BlockSpec((None, SUBLANES, cw), lambda b, j: (b, 0, j)),
                pl.BlockSpec((None, 1, cw), lambda b, j: (b, 0, j)),
                vec(cwid), vec(1),
                pl.BlockSpec((None, nbk, bw, bw), lambda b, j: (layer, j, 0, 0)), vec(1),
                pl.BlockSpec((None, nbk, bw, bw), lambda b, j: (layer, j, 0, 0)), vec(1), vec(1)]
    args = [zc, zc, bufp, h0, conv_w, conv_b, w_a, b_a, w_x, b_x, lam]
    aliases = {}
    if mix is None:
        y_shape = jax.ShapeDtypeStruct((NB * T, d_lru), F32)
    else:
        y_shape = jax.ShapeDtypeStruct(mix.shape, mix.dtype)
        in_specs.append(pl.BlockSpec(memory_space=pl.ANY))
        args.append(mix)
        aliases = {len(args) - 1: 0}
    return pl.pallas_call(
        kern,
        out_shape=(y_shape,
                   jax.ShapeDtypeStruct((NB, 1, d_lru), F32),
                   jax.ShapeDtypeStruct((NB, SUBLANES, d_lru), F32)),
        grid=(NB, ncg),
        in_specs=in_specs,
        out_specs=(pl.BlockSpec((T, cw), lambda b, j: (b, mix_col0 // cw + j)),
                   pl.BlockSpec((None, 1, cw), lambda b, j: (b, 0, j)),
                   pl.BlockSpec((None, SUBLANES, cw), lambda b, j: (b, 0, j))),
        scratch_shapes=[pltpu.VMEM((T + SUBLANES, cw), F32), pltpu.VMEM((nbk, T, bw), F32),
                        pltpu.VMEM((nbk, T, bw), F32)],
        input_output_aliases=aliases,
        compiler_params=_params(("parallel", "parallel")),
        name="rglru",
    )(*args)


def kernel(x_prompt, x_sample, cache_k, cache_v, state_gla, state_lru_h, state_lru_conv, page_table, meta_tokens, norm_mix_pre, norm_mix_post, norm_mlp_pre, norm_mlp_post, w_in, w_out, lam_q1, lam_k1, lam_q2, lam_k2, attn_subln, gla_w_gate, gla_b_gate, gla_norm, lru_conv_w, lru_conv_b, lru_w_a, lru_b_a, lru_w_x, lru_b_x, lru_lambda, mlp_w_up, mlp_w_down):
    B, seq, D = x_prompt.shape
    DB, S, _ = x_sample.shape
    nm = meta_tokens.shape[0]
    T = nm + seq
    depth = w_in.shape[0]
    H_A, dv_a = cache_k.shape[3], cache_v.shape[4]
    dh_a = dv_a // 2
    d_a = H_A * dv_a
    H_B, dk_b, dv_b = state_gla.shape[2:]
    d_b = H_B * dv_b
    rank = gla_w_gate.shape[1]
    d_lru = state_lru_h.shape[-1]
    conv_w = lru_conv_w.shape[1]
    d_ff = mlp_w_up.shape[2]
    R_P, R_S = B * T, DB * S
    assert nm <= LANES and nm % 16 == 0 and seq % LANES == 0 and S <= SUBLANES and R_P % S == 0
    assert dv_a == LANES and dk_b == LANES and dv_b % LANES == 0 and rank <= LANES

    c_ak, c_av = d_a, 2 * d_a
    c_bq = 3 * d_a
    c_bk = c_bq + H_B * dk_b
    c_bv = c_bk + H_B * dk_b
    c_bg = c_bv + d_b
    c_br = c_bg + d_b
    c_cx = c_br + rank
    n_ab = c_br
    gla_cols = (c_bq, c_bk, c_bv, c_bg)
    d_mix = d_a + d_b + d_lru

    w_in_t = jnp.swapaxes(w_in, 1, 2)
    wg = jnp.pad(gla_w_gate, ((0, 0), (0, LANES - rank), (0, 0)))
    row3 = lambda a: a.reshape(depth, 1, a.shape[-1])

    slopes = jnp.exp2(-8.0 * jnp.arange(1, H_A + 1, dtype=F32) / H_A)
    log2e = math.log2(math.e)
    slope_col = jnp.repeat(slopes * log2e, 2 * S).reshape(H_A * 2 * S, 1)
    eye_c = jnp.eye(2, dtype=F32)
    assert S * H_A <= LANES

    blocks_p = ((0, nm, LANES - nm),) + tuple((nm + LANES * n, LANES, 0) for n in range(seq // LANES))
    blocks_s = ((0, S, 0),)

    h = jnp.concatenate([piece for b in range(B) for piece in (meta_tokens, x_prompt[b])]
                        + [x_sample.reshape(R_S, D)], axis=0)
    zeros_gla = jnp.zeros((B, H_B, dk_b, dv_b), F32)
    zeros_h = jnp.zeros((B, 1, d_lru), F32)
    zeros_buf = jnp.zeros((B, SUBLANES, d_lru), F32)

    outs = [[] for _ in range(8)]
    zs = []
    hn = rms_cast(h, norm_mix_pre[0:1])
    for l in range(depth):
        lam_init = 0.8 - 0.6 * math.exp(-0.3 * l)
        lamp = jnp.stack([lam_q1[l], lam_k1[l], lam_q2[l], lam_k2[l]])
        subln = attn_subln[l:l + 1]

        tm = 2080
        z = matmul_nt(hn, w_in_t, l, 0, n_ab, tm_target=tm)
        zc = matmul_nt(hn, w_in_t, l, c_cx, 2 * d_lru, tm_target=tm)
        zr = matmul_nt(hn, w_in_t, l, c_br, rank)
        zs.append(z)

        gla_args = (wg, row3(gla_b_gate), row3(gla_norm))
        lru_args = (lru_conv_w, row3(lru_conv_b), lru_w_a, row3(lru_b_a), lru_w_x, row3(lru_b_x),
                    row3(lru_lambda))

        q_s = z[R_P:, :d_a].reshape(DB, S, H_A, 2, dh_a) * (dh_a ** -0.5 * log2e)
        qrows = jnp.einsum('bihcd,cy->bhciyd', q_s, eye_c).reshape(DB, H_A * 2 * S, dv_a).astype(BF16)
        new_rows = lambda c0: jnp.pad(z[R_P:, c0:c0 + d_a].reshape(DB, S * H_A, dv_a),
                                      ((0, 0), (0, LANES - S * H_A), (0, 0)))
        oa_s = attn_sample(qrows, new_rows(c_ak), new_rows(c_av), slope_col, lamp, subln, cache_k, cache_v,
                           page_table, l, DB=DB, S=S, H=H_A, dh=dh_a, lam_init=lam_init)
        ob_s, sg_s = gla(z, zr, *gla_args, state_gla[l], l, row0=R_P, NB=DB, T=S, H=H_B, dk=dk_b, dvb=dv_b,
                         cols=gla_cols, blocks=blocks_s)
        bufp = jnp.pad(state_lru_conv[l], ((0, 0), (SUBLANES - (conv_w - 1), 0), (0, 0)))
        oc_s, hl_s, nb_s = rglru(zc, bufp, state_lru_h[l][:, None, :], *lru_args, l, row0=R_P, NB=DB, T=S,
                                 d_lru=d_lru)

        mix = attn_prompt(z, slopes, lamp, subln, B=B, T=T, nm=nm, H=H_A, dh=dh_a, lam_init=lam_init,
                          mix_shape=(R_P + R_S, d_mix))
        mix, sg_p = gla(z, zr, *gla_args, zeros_gla, l, row0=0, NB=B, T=T, H=H_B, dk=dk_b, dvb=dv_b,
                        cols=gla_cols, blocks=blocks_p, mix=mix, mix_col0=d_a)
        mix, hl_p, nb_p = rglru(zc, zeros_buf, zeros_h, *lru_args, l, row0=0, NB=B, T=T, d_lru=d_lru,
                                mix=mix, mix_col0=d_a + d_b)
        mix = lax.dynamic_update_slice(mix, jnp.concatenate([oa_s, ob_s, oc_s], axis=1).astype(BF16), (R_P, 0))
        m1 = matmul(mix, w_out, l, D, F32, tm_target=tm)
        h, hn2 = resid_norm(h, m1, norm_mix_post[l:l + 1], norm_mlp_pre[l:l + 1])
        up = matmul(hn2, mlp_w_up, l, d_ff, BF16, relu2=True, tm_target=tm)
        dn = matmul_ktiled(up, mlp_w_down, l)
        if l + 1 < depth:
            h, hn = resid_norm(h, dn, norm_mlp_post[l:l + 1], norm_mix_pre[l + 1:l + 2])

        new = (z[R_P:, c_ak:c_av].reshape(DB, S, H_A, dv_a), z[R_P:, c_av:c_bq].reshape(DB, S, H_A, dv_a),
               sg_p, sg_s, hl_p[:, 0, :], hl_s[:, 0, :],
               nb_p[:, SUBLANES - (conv_w - 1):, :], nb_s[:, SUBLANES - (conv_w - 1):, :])
        for acc, val in zip(outs, new):
            acc.append(val)

    gpost = norm_mlp_post[depth - 1:depth]
    y_prompt = resid_prompt_out(h, dn, gpost, B=B, T=T, nm=nm)
    y_sample = resid_norm(h[R_P:], dn[R_P:], gpost)[0].reshape(DB, S, D)
    k_prompt, v_prompt = (a.reshape(depth, B, T, H_A, dv_a) for a in kv_export(zs, B=B, T=T, H=H_A, dv=dv_a))
    return (y_prompt, y_sample, k_prompt, v_prompt) + tuple(jnp.stack(o) for o in outs)
```

```python
import functools
import math

import jax
import jax.numpy as jnp
from jax import lax
from jax.experimental import pallas as pl
from jax.experimental.pallas import tpu as pltpu

F32 = jnp.float32
BF16 = jnp.bfloat16
EPS = 1e-6
GLA_TAU = 16.0
LRU_C = 8.0
NEG = -1e30
LANES = 128
SUBLANES = 8
VMEM_LIMIT_BYTES = 56 * 1024 * 1024

_NT = (((1,), (1,)), ((), ()))


def _tile(n, target, mult):
    best = None
    for d in range(mult, min(n, target) + 1, mult):
        if n % d == 0:
            best = d
    return n if best is None else best


def _params(sem):
    return pltpu.CompilerParams(dimension_semantics=sem, vmem_limit_bytes=VMEM_LIMIT_BYTES)


def _rms(x, g):
    return x * lax.rsqrt(jnp.mean(x * x, axis=-1, keepdims=True) + EPS) * g


def _embed_kernel(x_ref, g_ref, *rest):
    h_ref, hn_ref = rest[-2:]
    x = x_ref[...]
    h_ref[...] = x
    hn_ref[...] = _rms(x, g_ref[...]).astype(hn_ref.dtype)


def embed(x, g, prev, *, R, grid, in_spec, nrows, dst_row):
    D = x.shape[-1]
    out_spec = pl.BlockSpec((pl.Element(nrows), pl.Element(D)),
                            lambda *ix: (pl.multiple_of(dst_row(*ix), 16), 0))
    in_specs = [in_spec, pl.BlockSpec((1, D), lambda *ix: (0, 0))]
    args, aliases = [x, g], {}
    if prev is not None:
        in_specs += [pl.BlockSpec(memory_space=pl.ANY)] * 2
        args += list(prev)
        aliases = {2: 0, 3: 1}
    return pl.pallas_call(
        _embed_kernel, out_shape=(jax.ShapeDtypeStruct((R, D), F32), jax.ShapeDtypeStruct((R, D), BF16)),
        grid=grid, in_specs=in_specs, out_specs=(out_spec, out_spec), input_output_aliases=aliases,
        compiler_params=_params(("arbitrary",) * len(grid)), name="embed")(*args)


def _resid_kernel(h_ref, m_ref, gpost_ref, o_ref):
    o_ref[...] = h_ref[...] + _rms(m_ref[...], gpost_ref[...])


def _resid_next_kernel(h_ref, m_ref, gpost_ref, gpre_ref, o_ref, hn_ref):
    h = h_ref[...] + _rms(m_ref[...], gpost_ref[...])
    o_ref[...] = h
    hn_ref[...] = _rms(h, gpre_ref[...]).astype(hn_ref.dtype)


def resid_norm(h, m, gpost, gpre=None):
    R, D = h.shape
    tr = _tile(R, 208, 16)
    row = pl.BlockSpec((tr, D), lambda i: (i, 0))
    vec = pl.BlockSpec((1, D), lambda i: (0, 0))
    if gpre is None:
        return pl.pallas_call(
            _resid_kernel, out_shape=jax.ShapeDtypeStruct((R, D), F32), grid=(R // tr,),
            in_specs=[row, row, vec], out_specs=row, compiler_params=_params(("parallel",)),
            name="resid_norm")(h, m, gpost), None
    return pl.pallas_call(
        _resid_next_kernel,
        out_shape=(jax.ShapeDtypeStruct((R, D), F32), jax.ShapeDtypeStruct((R, D), BF16)),
        grid=(R // tr,), in_specs=[row, row, vec, vec], out_specs=(row, row),
        compiler_params=_params(("parallel",)), name="resid_norm_next")(h, m, gpost, gpre)


def resid_prompt_out(h, m, gpost, *, B, T, nm):
    D = h.shape[1]
    seq = T - nm
    tr = _tile(seq, 256, SUBLANES)
    rows = pl.BlockSpec((pl.Element(tr), pl.Element(D)),
                        lambda b, i: (pl.multiple_of(b * T + nm + i * tr, SUBLANES), 0))
    return pl.pallas_call(
        _resid_kernel, out_shape=jax.ShapeDtypeStruct((B, seq, D), F32), grid=(B, seq // tr),
        in_specs=[rows, rows, pl.BlockSpec((1, D), lambda b, i: (0, 0))],
        out_specs=pl.BlockSpec((None, tr, D), lambda b, i: (b, i, 0)),
        compiler_params=_params(("parallel", "parallel")), name="resid_prompt_out")(h, m, gpost)


def _kv_export_kernel(*refs, depth, H, dv):
    ko_ref, vo_ref = refs[-2:]
    for l in range(depth):
        @pl.when(pl.program_id(0) == l)
        def _(k_ref=refs[2 * l], v_ref=refs[2 * l + 1]):
            tt = k_ref.shape[0]
            for h in range(H):
                ko_ref[pl.ds(h, tt, stride=H), :] = k_ref[:, h * dv:(h + 1) * dv]
                vo_ref[pl.ds(h, tt, stride=H), :] = v_ref[:, h * dv:(h + 1) * dv]


def kv_export(zs, *, B, T, H, dv):
    depth = len(zs)
    tt = _tile(T, 344, SUBLANES)
    nt = T // tt
    last = B * nt - 1
    out = jax.ShapeDtypeStruct((depth * B * T * H, dv), F32)

    def z_spec(l, col):
        def idx(L, b, i):
            return jnp.where(L == l, b * nt + i, jnp.where(L < l, 0, last)), col
        return pl.BlockSpec((tt, H * dv), idx)

    out_spec = pl.BlockSpec((tt * H, dv), lambda L, b, i: ((L * B + b) * nt + i, 0))
    return pl.pallas_call(
        functools.partial(_kv_export_kernel, depth=depth, H=H, dv=dv), out_shape=(out, out),
        grid=(depth, B, nt),
        in_specs=[z_spec(l, col) for l in range(depth) for col in (1, 2)],
        out_specs=(out_spec, out_spec),
        compiler_params=_params(("arbitrary", "arbitrary", "arbitrary")), name="kv_export",
    )(*[z for z in zs for _ in range(2)])


def _mm_kernel(x_ref, w_ref, o_ref, *, relu2):
    acc = jnp.dot(x_ref[...], w_ref[...].astype(BF16), preferred_element_type=F32)
    if relu2:
        acc = jnp.square(jnp.maximum(acc, 0.0))
    o_ref[...] = acc.astype(o_ref.dtype)


def _x_block(tm, K, tm_target):
    mode = dict(pipeline_mode=pl.Buffered(1)) if tm_target > 1040 else {}
    return pl.BlockSpec((tm, K), lambda i, j: (i, 0), **mode)


def matmul(x, w, layer, n_out, out_dtype, relu2=False, tm_target=1040, tn_target=512):
    R, K = x.shape
    tm = _tile(R, tm_target, 16)
    tn = _tile(n_out, tn_target, LANES)
    return pl.pallas_call(
        functools.partial(_mm_kernel, relu2=relu2),
        out_shape=jax.ShapeDtypeStruct((R, n_out), out_dtype),
        grid=(R // tm, n_out // tn),
        in_specs=[_x_block(tm, K, tm_target),
                  pl.BlockSpec((None, K, tn), lambda i, j: (layer, 0, j))],
        out_specs=pl.BlockSpec((tm, tn), lambda i, j: (i, j)),
        compiler_params=_params(("parallel", "parallel")),
        name="matmul",
    )(x, w)


def _mm_nt_kernel(x_ref, w_ref, o_ref):
    w = w_ref[0].astype(BF16)
    pad = o_ref.shape[1] - w.shape[0]
    if pad:
        w = jnp.concatenate([w, jnp.zeros((pad, w.shape[1]), BF16)], axis=0)
    o_ref[...] = lax.dot_general(x_ref[...], w, _NT, preferred_element_type=F32)


def matmul_nt(x, wt, layer, row0, n_rows, tm_target=1040, tn_target=512):
    R, K = x.shape
    assert row0 % SUBLANES == 0
    tm = _tile(R, tm_target, 16)
    tn = _tile(n_rows, tn_target, LANES) if n_rows >= LANES else n_rows
    n_out = max(n_rows, LANES)
    to = max(tn, LANES)
    return pl.pallas_call(
        _mm_nt_kernel,
        out_shape=jax.ShapeDtypeStruct((R, n_out), F32),
        grid=(R // tm, n_rows // tn),
        in_specs=[_x_block(tm, K, tm_target),
                  pl.BlockSpec((pl.Element(1), pl.Element(tn), pl.Element(K)),
                               lambda i, j: (layer, pl.multiple_of(row0 + j * tn, SUBLANES), 0))],
        out_specs=pl.BlockSpec((tm, to), lambda i, j: (i, j)),
        compiler_params=_params(("parallel", "parallel")),
        name="matmul_nt",
    )(x, wt)


def _mmk_kernel(x_ref, w_ref, o_ref):
    @pl.when(pl.program_id(2) == 0)
    def _():
        o_ref[...] = jnp.zeros_like(o_ref)

    o_ref[...] += jnp.dot(x_ref[...], w_ref[...].astype(BF16), preferred_element_type=F32)


def matmul_ktiled(x, w, layer, tm_target=2080, tn_target=1024, tk_target=1024):
    R, K = x.shape
    N = w.shape[2]
    tm = _tile(R, tm_target, 16)
    tn = _tile(N, tn_target, LANES)
    tk = _tile(K, tk_target, LANES)
    return pl.pallas_call(
        _mmk_kernel,
        out_shape=jax.ShapeDtypeStruct((R, N), F32),
        grid=(R // tm, N // tn, K // tk),
        in_specs=[pl.BlockSpec((tm, tk), lambda i, j, k: (i, k)),
                  pl.BlockSpec((None, tk, tn), lambda i, j, k: (layer, k, j))],
        out_specs=pl.BlockSpec((tm, tn), lambda i, j, k: (i, j)),
        compiler_params=_params(("parallel", "parallel", "arbitrary")),
        name="matmul_ktiled",
    )(x, w)


def _lam_value(lamp_ref, lam_init):
    lp = lamp_ref[...]
    a = jnp.exp(jnp.sum(lp[0:1] * lp[1:2], axis=-1, keepdims=True))
    b = jnp.exp(jnp.sum(lp[2:3] * lp[3:4], axis=-1, keepdims=True))
    return a - b + lam_init


def _attn_prompt_kernel(slopes_ref, lamp_ref, subln_ref, subln_col_ref, q_ref, k_ref, v_ref, o_ref,
                        kb_ref, vt_ref, s_ref, *, nm, qb, nblk, dh, lam_init):
    T = q_ref.shape[0]
    dv = 2 * dh
    pad = LANES - nm
    log2e = math.log2(math.e)
    slope = slopes_ref[pl.program_id(1)] * log2e
    lam = _lam_value(lamp_ref, lam_init)
    scale = dh ** -0.5 * log2e

    lane = lax.broadcasted_iota(jnp.int32, (1, dv), 1)
    first_map = lane < dh

    kb_ref[0:pad, 0:dv] = jnp.zeros((pad, dv), BF16)
    kb_ref[pad:pad + T, 0:dv] = k_ref[...].astype(BF16)
    kbias = slope * (lax.broadcasted_iota(jnp.int32, (pad + T, dv), 0) - pad).astype(F32)
    hi = kbias.astype(BF16).astype(F32)
    mid = (kbias - hi).astype(BF16).astype(F32)
    lo = kbias - hi - mid
    kb_ref[:, dv:2 * dv] = jnp.where(lane == 0, hi, jnp.where(lane == 1, mid, jnp.where(lane == 2, lo, 0.0))
                                     ).astype(BF16)
    vt_ref[...] = jnp.concatenate([jnp.zeros((pad, dv), F32), v_ref[...]], axis=0).T.astype(BF16)
    ones3 = jnp.where(lane < 3, 1.0, 0.0)

    def stacked_q(q):
        qs = q * scale
        qq = jnp.concatenate([jnp.where(first_map, qs, 0.0), jnp.where(first_map, 0.0, qs)], axis=0)
        return jnp.concatenate([qq, jnp.broadcast_to(ones3, qq.shape)], axis=1).astype(BF16)

    qq = stacked_q(q_ref[0:nm, :])
    s = lax.dot_general(qq, kb_ref[0:LANES, :], _NT, preferred_element_type=F32)
    col = lax.broadcasted_iota(jnp.int32, (2 * nm, LANES), 1)
    row = lax.broadcasted_iota(jnp.int32, (2 * nm, LANES), 0)
    row = jnp.where(row >= nm, row - nm, row)
    kpos = col - pad
    s = jnp.where((kpos >= 0) & (kpos <= row), s, NEG)
    m = jnp.max(s, axis=-1, keepdims=True)
    p = jnp.exp2(s - m)
    l = jnp.sum(p, axis=-1, keepdims=True)
    o = lax.dot_general(p.astype(BF16), vt_ref[:, 0:LANES], _NT, preferred_element_type=F32) / l
    res = o[:nm] - lam * o[nm:]
    o_ref[0:nm, :] = (_rms(res, subln_ref[...]) * (1.0 - lam_init)).astype(o_ref.dtype)

    key_ok = lax.broadcasted_iota(jnp.int32, (LANES, 2 * qb), 0) >= pad
    ql = lax.broadcasted_iota(jnp.int32, (qb, 2 * qb), 1)
    ql = jnp.where(ql >= qb, ql - qb, ql)
    causal = lax.broadcasted_iota(jnp.int32, (qb, 2 * qb), 0) <= ql

    for qj in range(nblk):
        r0 = nm + qj * qb
        nk = LANES + (qj + 1) * qb
        qq = stacked_q(q_ref[r0:r0 + qb, :])
        s = lax.dot_general(kb_ref[0:nk, :], qq, _NT, preferred_element_type=F32)
        parts = [jnp.where(key_ok, s[0:LANES, :], NEG)]
        if nk - qb > LANES:
            parts.append(s[LANES:nk - qb, :])
        parts.append(jnp.where(causal, s[nk - qb:nk, :], NEG))
        s_ref[0:nk, :] = jnp.concatenate(parts, axis=0)
        m = jnp.max(s_ref[0:nk, :], axis=0, keepdims=True)
        p = jnp.exp2(s_ref[0:nk, :] - m)
        l = jnp.sum(p, axis=0, keepdims=True)
        o = jnp.dot(vt_ref[:, 0:nk], p.astype(BF16), preferred_element_type=F32) / l
        res = o[:, :qb] - lam * o[:, qb:]
        y = res * lax.rsqrt(jnp.mean(res * res, axis=0, keepdims=True) + EPS) * subln_col_ref[...]
        o_ref[r0:r0 + qb, :] = (y * (1.0 - lam_init)).T.astype(o_ref.dtype)


def attn_prompt(z, slopes, lamp, subln, *, B, T, nm, H, dh, lam_init, mix_shape):
    seq = T - nm
    qb = _tile(seq, 512, LANES)
    nblk = seq // qb
    dv = 2 * dh
    kern = functools.partial(_attn_prompt_kernel, nm=nm, qb=qb, nblk=nblk, dh=dh, lam_init=lam_init)
    return pl.pallas_call(
        kern,
        out_shape=jax.ShapeDtypeStruct(mix_shape, BF16),
        grid=(B, H),
        in_specs=[pl.BlockSpec(memory_space=pltpu.SMEM),
                  pl.BlockSpec((4, dh), lambda b, h: (0, 0)),
                  pl.BlockSpec((1, dv), lambda b, h: (0, 0)),
                  pl.BlockSpec((dv, 1), lambda b, h: (0, 0)),
                  pl.BlockSpec((T, dv), lambda b, h: (b, h)),
                  pl.BlockSpec((T, dv), lambda b, h: (b, H + h)),
                  pl.BlockSpec((T, dv), lambda b, h: (b, 2 * H + h))],
        out_specs=pl.BlockSpec((T, dv), lambda b, h: (b, h)),
        scratch_shapes=[pltpu.VMEM((LANES + seq, 2 * dv), BF16), pltpu.VMEM((dv, LANES + seq), BF16),
                        pltpu.VMEM((LANES + seq, 2 * qb), F32)],
        compiler_params=_params(("parallel", "parallel")),
        name="attn_prompt",
    )(slopes, lamp, subln, subln.reshape(dv, 1), z, z, z)


def _attn_sample_kernel(pt_ref, lamp_ref, subln_ref, slope_ref, q_ref, kn_ref, vn_ref, *rest,
                        G, page, past_len, H, S, dv, lam_init):
    k_refs, v_refs = rest[:G], rest[G:2 * G]
    o_ref, m_ref, l_ref, acc_ref, bias_ref = rest[2 * G:]
    j = pl.program_id(1)
    nrow = H * 2 * S
    h_shift, s_shift = H.bit_length() - 1, (2 * S).bit_length() - 1
    row = lax.broadcasted_iota(jnp.int32, (nrow, 1), 0)
    row_head = lax.shift_right_logical(row, s_shift)
    slope = slope_ref[...]

    def block_bias(n, limit=None):
        col = lax.broadcasted_iota(jnp.int32, (1, n), 1)
        key = lax.shift_right_logical(col, h_shift)
        ok = (col & (H - 1)) == row_head
        if limit is not None:
            ok = ok & (key <= limit)
        return jnp.where(ok, slope * key.astype(F32), NEG)

    @pl.when(j == 0)
    def _():
        m_ref[...] = jnp.full(m_ref.shape, NEG, F32)
        l_ref[...] = jnp.zeros(l_ref.shape, F32)
        acc_ref[...] = jnp.zeros(acc_ref.shape, F32)
        bias_ref[...] = block_bias(bias_ref.shape[1])

    def update(k_parts, v_parts, bias, shift):
        q = q_ref[...]
        s = jnp.concatenate([lax.dot_general(q, r[...].astype(BF16), _NT, preferred_element_type=F32)
                             for r in k_parts], axis=1) + bias
        m_old = m_ref[...]
        m_new = jnp.maximum(m_old, jnp.max(s, axis=-1, keepdims=True) + shift)
        alpha = jnp.exp2(m_old - m_new)
        p = jnp.exp2(s - (m_new - shift))
        l_ref[...] = alpha * l_ref[...] + jnp.sum(p, axis=-1, keepdims=True)
        p = p.astype(BF16)
        pv, c0 = None, 0
        for r in v_parts:
            d = jnp.dot(p[:, c0:c0 + r.shape[0]], r[...].astype(BF16), preferred_element_type=F32)
            pv = d if pv is None else pv + d
            c0 += r.shape[0]
        acc_ref[...] = alpha * acc_ref[...] + pv
        m_ref[...] = m_new

    update(k_refs, v_refs, bias_ref[...], slope * (j * (G * page) - past_len).astype(F32))

    @pl.when(j == pl.num_programs(1) - 1)
    def _():
        update([kn_ref], [vn_ref], block_bias(kn_ref.shape[0], row & (S - 1)), 0.0)
        lam = _lam_value(lamp_ref, lam_init)
        o = acc_ref[...] / l_ref[...]
        for h in range(H):
            blk = o[h * 2 * S:(h + 1) * 2 * S, :]
            res = blk[:S] - lam * blk[S:]
            o_ref[:, h * dv:(h + 1) * dv] = _rms(res, subln_ref[...]) * (1.0 - lam_init)


def attn_sample(qrows, knew, vnew, slope_col, lamp, subln, cache_k, cache_v, page_table, layer, *,
                DB, S, H, dh, lam_init):
    dv = 2 * dh
    n_pages = page_table.shape[1]
    page = cache_k.shape[2]
    assert H & (H - 1) == 0 and S & (S - 1) == 0
    G = _tile(n_pages, 8, 1)
    ck = cache_k.reshape(cache_k.shape[0], cache_k.shape[1], page * H, dv)
    cv = cache_v.reshape(cache_v.shape[0], cache_v.shape[1], page * H, dv)
    nrow = H * 2 * S

    def page_spec(g):
        return pl.BlockSpec((None, None, page * H, dv), lambda b, j, pt: (layer, pt[b, j * G + g], 0, 0))

    kern = functools.partial(_attn_sample_kernel, G=G, page=page, past_len=n_pages * page,
                             H=H, S=S, dv=dv, lam_init=lam_init)
    per_b = lambda rows: pl.BlockSpec((None, rows, dv), lambda b, j, pt: (b, 0, 0))
    grid_spec = pltpu.PrefetchScalarGridSpec(
        num_scalar_prefetch=1,
        grid=(DB, n_pages // G),
        in_specs=[pl.BlockSpec((4, dh), lambda b, j, pt: (0, 0)),
                  pl.BlockSpec((1, dv), lambda b, j, pt: (0, 0)),
                  pl.BlockSpec((nrow, 1), lambda b, j, pt: (0, 0)),
                  per_b(nrow), per_b(knew.shape[1]), per_b(vnew.shape[1])]
                 + [page_spec(g) for g in range(G)] + [page_spec(g) for g in range(G)],
        out_specs=pl.BlockSpec((S, H * dv), lambda b, j, pt: (b, 0)),
        scratch_shapes=[pltpu.VMEM((nrow, 1), F32), pltpu.VMEM((nrow, 1), F32),
                        pltpu.VMEM((nrow, dv), F32), pltpu.VMEM((nrow, G * page * H), F32)],
    )
    return pl.pallas_call(
        kern, out_shape=jax.ShapeDtypeStruct((DB * S, H * dv), F32), grid_spec=grid_spec,
        compiler_params=_params(("parallel", "arbitrary")), name="attn_sample",
    )(page_table, lamp, subln, slope_col, qrows, knew, vnew, *([ck] * G), *([cv] * G))


def _gla_kernel(q_ref, k_ref, v_ref, g_ref, r_ref, wg_ref, bg_ref, norm_ref, s0_ref, *rest,
                blocks, dk, dvb, hp):
    o_ref, sout_ref = rest[-2:]
    L = LANES
    ri = lax.broadcasted_iota(jnp.int32, (L, L), 0)
    ci = lax.broadcasted_iota(jnp.int32, (L, L), 1)
    tril = ri >= ci
    tril_bf = jnp.where(tril, 1.0, 0.0).astype(BF16)
    rowi = lax.broadcasted_iota(jnp.int32, (L, 1), 0)
    qscale = dk ** -0.5
    states = [s0_ref[hh] for hh in range(hp)]

    def load(ref, row0, nreal, off, c0, width):
        x = ref[row0:row0 + nreal, c0:c0 + width]
        if nreal == L:
            return x
        parts = []
        if off:
            parts.append(jnp.zeros((off, width), x.dtype))
        parts.append(x)
        if L - off - nreal:
            parts.append(jnp.zeros((L - off - nreal, width), x.dtype))
        return jnp.concatenate(parts, axis=0)

    r_all = jnp.concatenate([load(r_ref, *blk, 0, L) for blk in blocks], axis=0).astype(BF16)
    x_all = jnp.dot(r_all, wg_ref[...].astype(BF16), preferred_element_type=F32) + bg_ref[...]
    gk_all = (jnp.minimum(x_all, 0.0) - jnp.log1p(jnp.exp(-jnp.abs(x_all)))) * (1.0 / GLA_TAU)
    pieces = []
    for n, (row0, nreal, off) in enumerate(blocks):
        g_n = gk_all[n * L:(n + 1) * L, :]
        if nreal < L:
            g_n = jnp.where((rowi >= off) & (rowi < off + nreal), g_n, 0.0)
        pieces.append(g_n)
    g_wide = jnp.concatenate(pieces, axis=1)
    hi = g_wide.astype(BF16)
    lo = (g_wide - hi.astype(F32)).astype(BF16)
    b_wide = (jnp.dot(tril_bf, hi, preferred_element_type=F32)
              + jnp.dot(tril_bf, lo, preferred_element_type=F32))

    for n, (row0, nreal, off) in enumerate(blocks):
        for hh in range(hp):
            S = states[hh]
            q = load(q_ref, row0, nreal, off, hh * dk, dk) * qscale
            k = load(k_ref, row0, nreal, off, hh * dk, dk)
            v = load(v_ref, row0, nreal, off, hh * dvb, dvb).astype(BF16)
            b = b_wide[:, (n * hp + hh) * dk:(n * hp + hh + 1) * dk]
            b_last = b[L - 1:L, :]
            b_mid = b[L // 2 - 1:L // 2, :]
            qs = (q * jnp.exp(b)).astype(BF16)
            qm = (q * jnp.exp(b - b_mid)).astype(BF16)
            km = (k * jnp.exp(b_mid - b)).astype(BF16)
            att = lax.dot_general(qm, km, _NT, preferred_element_type=F32)
            att = jnp.where(tril, att, 0.0).astype(BF16)
            o = jnp.dot(jnp.concatenate([att, qs], axis=1),
                        jnp.concatenate([v, S.astype(BF16)], axis=0), preferred_element_type=F32)
            kk = (k * jnp.exp(b_last - b)).T.astype(BF16)
            decay = jnp.broadcast_to(jnp.exp(b_last), (L, dk)).T
            states[hh] = (jnp.concatenate([decay] * (dvb // L), axis=1) * S
                          + jnp.dot(kk, v, preferred_element_type=F32))
            y = _rms(o, norm_ref[...])
            g = load(g_ref, row0, nreal, off, hh * dvb, dvb)
            y = y * (g / (1.0 + jnp.exp(-g)))
            o_ref[row0:row0 + nreal, hh * dvb:(hh + 1) * dvb] = y[off:off + nreal, :].astype(o_ref.dtype)
    for hh in range(hp):
        sout_ref[hh] = states[hh]


def gla(z, zr, wg, bgate, norm, s0, layer, *, row0, NB, T, H, dk, dvb, cols, blocks, mix=None, mix_col0=0):
    hp = 2 if H % 2 == 0 else 1
    cq, ck_, cv_, cg_ = cols
    wq, wv = hp * dk, hp * dvb
    assert cq % wq == 0 and ck_ % wq == 0 and cv_ % wv == 0 and cg_ % wv == 0 and mix_col0 % wv == 0
    rb = row0 // T
    kern = functools.partial(_gla_kernel, blocks=blocks, dk=dk, dvb=dvb, hp=hp)
    in_specs = [pl.BlockSpec((T, wq), lambda b, g: (rb + b, cq // wq + g)),
                pl.BlockSpec((T, wq), lambda b, g: (rb + b, ck_ // wq + g)),
                pl.BlockSpec((T, wv), lambda b, g: (rb + b, cv_ // wv + g)),
                pl.BlockSpec((T, wv), lambda b, g: (rb + b, cg_ // wv + g)),
                pl.BlockSpec((T, LANES), lambda b, g: (rb + b, 0)),
                pl.BlockSpec((None, LANES, wq), lambda b, g: (layer, 0, g)),
                pl.BlockSpec((None, 1, wq), lambda b, g: (layer, 0, g)),
                pl.BlockSpec((None, 1, dvb), lambda b, g: (layer, 0, 0)),
                pl.BlockSpec((None, hp, dk, dvb), lambda b, g: (b, g, 0, 0))]
    args = [z, z, z, z, zr, wg, bgate, norm, s0]
    aliases = {}
    if mix is None:
        y_shape = jax.ShapeDtypeStruct((NB * T, H * dvb), F32)
    else:
        y_shape = jax.ShapeDtypeStruct(mix.shape, mix.dtype)
        in_specs.append(pl.BlockSpec(memory_space=pl.ANY))
        args.append(mix)
        aliases = {len(args) - 1: 0}
    return pl.pallas_call(
        kern,
        out_shape=(y_shape, jax.ShapeDtypeStruct((NB, H, dk, dvb), F32)),
        grid=(NB, H // hp),
        in_specs=in_specs,
        out_specs=(pl.BlockSpec((T, wv), lambda b, g: (b, mix_col0 // wv + g)),
                   pl.BlockSpec((None, hp, dk, dvb), lambda b, g: (b, g, 0, 0))),
        input_output_aliases=aliases,
        compiler_params=_params(("parallel", "parallel")),
        name="gla",
    )(*args)


def _lru_kernel(cx_ref, cg_ref, buf_ref, h0_ref, cw_ref, cb_ref, wa_ref, ba_ref, wx_ref, bx_ref, lam_ref,
                *rest, T, cw, bw, conv_w):
    y_ref, hl_ref, nb_ref, e_ref, a_ref, x_ref = rest[-6:]
    seg = T // SUBLANES
    e_ref[0:SUBLANES, :] = buf_ref[...]
    e_ref[SUBLANES:SUBLANES + T, :] = cx_ref[...]
    first = SUBLANES - (conv_w - 1)
    u = cb_ref[...] + cw_ref[0:1, :] * e_ref[first:first + T, :]
    for j in range(1, conv_w):
        u = u + cw_ref[j:j + 1, :] * e_ref[first + j:first + j + T, :]
    nb_ref[...] = e_ref[T:T + SUBLANES, :]

    lam = lam_ref[...]
    decay = (-LRU_C) * (jnp.maximum(-lam, 0.0) + jnp.log1p(jnp.exp(-jnp.abs(lam))))
    for n in range(cw // bw):
        sl = slice(n * bw, (n + 1) * bw)
        un = u[:, sl]
        ub = un.astype(BF16)
        r = jax.nn.sigmoid(jnp.dot(ub, wa_ref[n].astype(BF16), preferred_element_type=F32) + ba_ref[:, sl])
        i = jax.nn.sigmoid(jnp.dot(ub, wx_ref[n].astype(BF16), preferred_element_type=F32) + bx_ref[:, sl])
        log_a = r * decay[:, sl]
        a = jnp.exp(log_a)
        a_ref[n] = a
        y = (1.0 - a) * (1.0 + a)
        x_ref[n] = jnp.where(y > 0.0, y * lax.rsqrt(y), 0.0) * (i * un)

    nl = cw // bw
    rows = lambda t: pl.ds(t, SUBLANES, stride=seg)
    sub = lax.broadcasted_iota(jnp.int32, (SUBLANES, bw), 0)
    bcast = lambda v, s: jnp.broadcast_to(v[s:s + 1, :], (SUBLANES, bw))

    def pass1(t, carry):
        hs, ps = carry
        new_h, new_p = [], []
        for n in range(nl):
            a8 = a_ref[n, rows(t), :]
            h = a8 * hs[n] + x_ref[n, rows(t), :]
            p = a8 * ps[n]
            x_ref[n, rows(t), :] = h
            a_ref[n, rows(t), :] = p
            new_h.append(h)
            new_p.append(p)
        return tuple(new_h), tuple(new_p)

    h_init = tuple(jnp.where(sub == 0, jnp.broadcast_to(h0_ref[:, n * bw:(n + 1) * bw], (SUBLANES, bw)), 0.0)
                   for n in range(nl))
    p_init = tuple(jnp.ones((SUBLANES, bw), F32) for _ in range(nl))
    h_end, p_end = lax.fori_loop(0, seg, pass1, (h_init, p_init))

    carry_in = []
    for n in range(nl):
        e = bcast(h_end[n], 0)
        c = jnp.zeros((SUBLANES, bw), F32)
        for s in range(1, SUBLANES):
            c = jnp.where(sub == s, e, c)
            e = bcast(h_end[n], s) + bcast(p_end[n], s) * e
        carry_in.append(c)
        hl_ref[:, n * bw:(n + 1) * bw] = e[0:1, :]

    def pass2(t, _):
        for n in range(nl):
            x_ref[n, rows(t), :] = x_ref[n, rows(t), :] + a_ref[n, rows(t), :] * carry_in[n]
        return 0

    lax.fori_loop(0, seg, pass2, 0)
    for n in range(nl):
        sl = slice(n * bw, (n + 1) * bw)
        g = cg_ref[:, sl]
        a1 = -2.0 * math.sqrt(2.0 / math.pi) * math.log2(math.e)
        gelu = g / (1.0 + jnp.exp2(g * (a1 + (a1 * 0.044715) * (g * g))))
        y_ref[:, sl] = (x_ref[n] * gelu).astype(y_ref.dtype)


def rglru(zc, bufp, h0, conv_w, conv_b, w_a, b_a, w_x, b_x, lam, layer, *, row0, NB, T, d_lru,
          mix=None, mix_col0=0):
    bw = w_a.shape[2]
    assert bw == LANES and T % SUBLANES == 0
    cw = _tile(d_lru, 512, bw)
    ncg = d_lru // cw
    nbk = cw // bw
    cwid = conv_w.shape[1]
    rb = row0 // T
    assert mix_col0 % cw == 0
    kern = functools.partial(_lru_kernel, T=T, cw=cw, bw=bw, conv_w=cwid)
    vec = lambda rows: pl.BlockSpec((None, rows, cw), lambda b, j: (layer, 0, j))
    in_specs = [pl.BlockSpec((T, cw), lambda b, j: (rb + b, j)),
                pl.BlockSpec((T, cw), lambda b, j: (rb + b, ncg + j)),
                pl.BlockSpec((None, SUBLANES, cw), lambda b, j: (b, 0, j)),
                pl.BlockSpec((None, 1, cw), lambda b, j: (b, 0, j)),
                vec(cwid), vec(1),
                pl.BlockSpec((None, nbk, bw, bw), lambda b, j: (layer, j, 0, 0)), vec(1),
                pl.BlockSpec((None, nbk, bw, bw), lambda b, j: (layer, j, 0, 0)), vec(1), vec(1)]
    args = [zc, zc, bufp, h0, conv_w, conv_b, w_a, b_a, w_x, b_x, lam]
    aliases = {}
    if mix is None:
        y_shape = jax.ShapeDtypeStruct((NB * T, d_lru), F32)
    else:
        y_shape = jax.ShapeDtypeStruct(mix.shape, mix.dtype)
        in_specs.append(pl.BlockSpec(memory_space=pl.ANY))
        args.append(mix)
        aliases = {len(args) - 1: 0}
    return pl.pallas_call(
        kern,
        out_shape=(y_shape,
                   jax.ShapeDtypeStruct((NB, 1, d_lru), F32),
                   jax.ShapeDtypeStruct((NB, SUBLANES, d_lru), F32)),
        grid=(NB, ncg),
        in_specs=in_specs,
        out_specs=(pl.BlockSpec((T, cw), lambda b, j: (b, mix_col0 // cw + j)),
                   pl.BlockSpec((None, 1, cw), lambda b, j: (b, 0, j)),
                   pl.BlockSpec((None, SUBLANES, cw), lambda b, j: (b, 0, j))),
        scratch_shapes=[pltpu.VMEM((T + SUBLANES, cw), F32), pltpu.VMEM((nbk, T, bw), F32),
                        pltpu.VMEM((nbk, T, bw), F32)],
        input_output_aliases=aliases,
        compiler_params=_params(("parallel", "parallel")),
        name="rglru",
    )(*args)


def kernel(x_prompt, x_sample, cache_k, cache_v, state_gla, state_lru_h, state_lru_conv, page_table, meta_tokens, norm_mix_pre, norm_mix_post, norm_mlp_pre, norm_mlp_post, w_in, w_out, lam_q1, lam_k1, lam_q2, lam_k2, attn_subln, gla_w_gate, gla_b_gate, gla_norm, lru_conv_w, lru_conv_b, lru_w_a, lru_b_a, lru_w_x, lru_b_x, lru_lambda, mlp_w_up, mlp_w_down):
    B, seq, D = x_prompt.shape
    DB, S, _ = x_sample.shape
    nm = meta_tokens.shape[0]
    T = nm + seq
    depth = w_in.shape[0]
    H_A, dv_a = cache_k.shape[3], cache_v.shape[4]
    dh_a = dv_a // 2
    d_a = H_A * dv_a
    H_B, dk_b, dv_b = state_gla.shape[2:]
    d_b = H_B * dv_b
    rank = gla_w_gate.shape[1]
    d_lru = state_lru_h.shape[-1]
    conv_w = lru_conv_w.shape[1]
    d_ff = mlp_w_up.shape[2]
    R_P, R_S = B * T, DB * S
    assert nm <= LANES and nm % 16 == 0 and seq % LANES == 0 and S <= SUBLANES and R_P % S == 0
    assert dv_a == LANES and dk_b == LANES and dv_b % LANES == 0 and rank <= LANES

    c_ak, c_av = d_a, 2 * d_a
    c_bq = 3 * d_a
    c_bk = c_bq + H_B * dk_b
    c_bv = c_bk + H_B * dk_b
    c_bg = c_bv + d_b
    c_br = c_bg + d_b
    c_cx = c_br + rank
    n_ab = c_br
    gla_cols = (c_bq, c_bk, c_bv, c_bg)
    d_mix = d_a + d_b + d_lru

    w_in_t = jnp.swapaxes(w_in, 1, 2)
    wg = jnp.pad(gla_w_gate, ((0, 0), (0, LANES - rank), (0, 0)))
    row3 = lambda a: a.reshape(depth, 1, a.shape[-1])

    slopes = jnp.exp2(-8.0 * jnp.arange(1, H_A + 1, dtype=F32) / H_A)
    log2e = math.log2(math.e)
    slope_col = jnp.repeat(slopes * log2e, 2 * S).reshape(H_A * 2 * S, 1)
    eye_c = jnp.eye(2, dtype=F32)
    assert S * H_A <= LANES

    blocks_p = ((0, nm, LANES - nm),) + tuple((nm + LANES * n, LANES, 0) for n in range(seq // LANES))
    blocks_s = ((0, S, 0),)

    g0 = norm_mix_pre[0:1]
    te = _tile(seq, 256, 16)
    hs = embed(x_prompt, g0, None, R=R_P + R_S, grid=(B, seq // te),
               in_spec=pl.BlockSpec((None, te, D), lambda b, i: (b, i, 0)), nrows=te,
               dst_row=lambda b, i: b * T + nm + i * te)
    hs = embed(meta_tokens, g0, hs, R=R_P + R_S, grid=(B,),
               in_spec=pl.BlockSpec((nm, D), lambda b: (0, 0)), nrows=nm, dst_row=lambda b: b * T)
    h, hn = embed(x_sample.reshape(R_S, D), g0, hs, R=R_P + R_S, grid=(1,),
                  in_spec=pl.BlockSpec((R_S, D), lambda i: (0, 0)), nrows=R_S, dst_row=lambda i: R_P + 0 * i)
    zeros_gla = jnp.zeros((B, H_B, dk_b, dv_b), F32)
    zeros_h = jnp.zeros((B, 1, d_lru), F32)
    zeros_buf = jnp.zeros((B, SUBLANES, d_lru), F32)

    outs = [[] for _ in range(8)]
    zs = []
    for l in range(depth):
        lam_init = 0.8 - 0.6 * math.exp(-0.3 * l)
        lamp = jnp.stack([lam_q1[l], lam_k1[l], lam_q2[l], lam_k2[l]])
        subln = attn_subln[l:l + 1]

        tm = 2080
        z = matmul_nt(hn, w_in_t, l, 0, n_ab, tm_target=tm)
        zc = matmul_nt(hn, w_in_t, l, c_cx, 2 * d_lru, tm_target=tm)
        zr = matmul_nt(hn, w_in_t, l, c_br, rank)
        zs.append(z)

        gla_args = (wg, row3(gla_b_gate), row3(gla_norm))
        lru_args = (lru_conv_w, row3(lru_conv_b), lru_w_a, row3(lru_b_a), lru_w_x, row3(lru_b_x),
                    row3(lru_lambda))

        q_s = z[R_P:, :d_a].reshape(DB, S, H_A, 2, dh_a) * (dh_a ** -0.5 * log2e)
        qrows = jnp.einsum('bihcd,cy->bhciyd', q_s, eye_c).reshape(DB, H_A * 2 * S, dv_a).astype(BF16)
        new_rows = lambda c0: jnp.pad(z[R_P:, c0:c0 + d_a].reshape(DB, S * H_A, dv_a),
                                      ((0, 0), (0, LANES - S * H_A), (0, 0)))
        oa_s = attn_sample(qrows, new_rows(c_ak), new_rows(c_av), slope_col, lamp, subln, cache_k, cache_v,
                           page_table, l, DB=DB, S=S, H=H_A, dh=dh_a, lam_init=lam_init)
        ob_s, sg_s = gla(z, zr, *gla_args, state_gla[l], l, row0=R_P, NB=DB, T=S, H=H_B, dk=dk_b, dvb=dv_b,
                         cols=gla_cols, blocks=blocks_s)
        bufp = jnp.pad(state_lru_conv[l], ((0, 0), (SUBLANES - (conv_w - 1), 0), (0, 0)))
        oc_s, hl_s, nb_s = rglru(zc, bufp, state_lru_h[l][:, None, :], *lru_args, l, row0=R_P, NB=DB, T=S,
                                 d_lru=d_lru)

        mix = attn_prompt(z, slopes, lamp, subln, B=B, T=T, nm=nm, H=H_A, dh=dh_a, lam_init=lam_init,
                          mix_shape=(R_P + R_S, d_mix))
        mix, sg_p = gla(z, zr, *gla_args, zeros_gla, l, row0=0, NB=B, T=T, H=H_B, dk=dk_b, dvb=dv_b,
                        cols=gla_cols, blocks=blocks_p, mix=mix, mix_col0=d_a)
        mix, hl_p, nb_p = rglru(zc, zeros_buf, zeros_h, *lru_args, l, row0=0, NB=B, T=T, d_lru=d_lru,
                                mix=mix, mix_col0=d_a + d_b)
        mix = lax.dynamic_update_slice(mix, jnp.concatenate([oa_s, ob_s, oc_s], axis=1).astype(BF16), (R_P, 0))
        m1 = matmul(mix, w_out, l, D, F32, tm_target=tm)
        h, hn2 = resid_norm(h, m1, norm_mix_post[l:l + 1], norm_mlp_pre[l:l + 1])
        up = matmul(hn2, mlp_w_up, l, d_ff, BF16, relu2=True, tm_target=tm)
        dn = matmul_ktiled(up, mlp_w_down, l)
        if l + 1 < depth:
            h, hn = resid_norm(h, dn, norm_mlp_post[l:l + 1], norm_mix_pre[l + 1:l + 2])

        new = (z[R_P:, c_ak:c_av].reshape(DB, S, H_A, dv_a), z[R_P:, c_av:c_bq].reshape(DB, S, H_A, dv_a),
               sg_p, sg_s, hl_p[:, 0, :], hl_s[:, 0, :],
               nb_p[:, SUBLANES - (conv_w - 1):, :], nb_s[:, SUBLANES - (conv_w - 1):, :])
        for acc, val in zip(outs, new):
            acc.append(val)

    gpost = norm_mlp_post[depth - 1:depth]
    y_prompt = resid_prompt_out(h, dn, gpost, B=B, T=T, nm=nm)
    y_sample = resid_norm(h[R_P:], dn[R_P:], gpost)[0].reshape(DB, S, D)
    k_prompt, v_prompt = (a.reshape(depth, B, T, H_A, dv_a) for a in kv_export(zs, B=B, T=T, H=H_A, dv=dv_a))
    return (y_prompt, y_sample, k_prompt, v_prompt) + tuple(jnp.stack(o) for o in outs)
```
